```python
import jax
import jax.numpy as jnp
from jax import lax
import numpy as np


D_MODEL = 1024
BATCH = 8
SEQ = 4096
DEPTH = 2

CHUNK = 64
NORM_EPS = 1e-6
GLA_HEADS = 4
GLA_DK = 64
GLA_DV = 128
GLA_GATE_RANK = 16
GLA_GATE_TAU = 16.0
GDN_HEADS = 4
GDN_DK = 128
GDN_DV = 128
GDN_CONV = 4
SSD_HEADS = 8
SSD_HEAD_DIM = 64
SSD_GROUPS = 2
SSD_STATE = 64
SSD_CONV = 4
SSD_INNER = SSD_HEADS * SSD_HEAD_DIM
N_BRANCH = 3
BRANCH_WIDTH = 512
N_EXPERTS = 32
TOP_K = 4
D_FF_EXPERT = D_MODEL
SWIGLU_LIMIT = 7.0
SWIGLU_ALPHA = 1.702
MOE_BLOCK = 256

IN_SPLITS = (
    GLA_HEADS * GLA_DK,
    GLA_HEADS * GLA_DK,
    GLA_HEADS * GLA_DV,
    GLA_GATE_RANK,
    GLA_HEADS * GLA_DV,
    GDN_HEADS * (2 * GDN_DK + GDN_DV),
    GDN_HEADS,
    GDN_HEADS,
    GDN_HEADS * GDN_DV,
    SSD_INNER,
    SSD_INNER + 2 * SSD_GROUPS * SSD_STATE,
    SSD_HEADS,
    N_BRANCH * D_MODEL,
)
IN_COLS = sum(IN_SPLITS)

kernel_name = 'hybrid_gla_gdn_ssd_moe_adaln'


def _split_points():
    pts, acc = [], 0
    for s in IN_SPLITS[:-1]:
        acc += s
        pts.append(acc)
    return pts


def _rms(x):
    xf = x.astype(jnp.float32)
    return xf * lax.rsqrt(jnp.mean(xf * xf, axis=-1, keepdims=True) + NORM_EPS)


def rmsnorm(x, g):
    return (_rms(x) * g.astype(jnp.float32)).astype(x.dtype)


def l2norm(x):
    return x * lax.rsqrt(jnp.sum(x * x, axis=-1, keepdims=True) + 1e-6)


def causal_dwconv(u, w, b=None):
    k = w.shape[0]
    ch = u.shape[-1]
    out = lax.conv_general_dilated(
        u, w.astype(u.dtype)[:, None, :], window_strides=(1,), padding=[(k - 1, 0)],
        dimension_numbers=('NWC', 'WIO', 'NWC'), feature_group_count=ch)
    if b is not None:
        out = out + b.astype(u.dtype)
    return out


def to_heads(u, n_heads):
    bsz, seq, width = u.shape
    return u.reshape(bsz, seq // CHUNK, CHUNK, n_heads, width // n_heads).transpose(0, 3, 1, 2, 4)


def heads_scalar(u):
    bsz, seq, nh = u.shape
    return u.reshape(bsz, seq // CHUNK, CHUNK, nh).transpose(0, 3, 1, 2)


def from_heads(o):
    bsz, nh, nc, cl, d = o.shape
    return o.transpose(0, 2, 3, 1, 4).reshape(bsz, nc * cl, nh * d)


def chunk_masks():
    ones = jnp.ones((CHUNK, CHUNK), dtype=bool)
    return jnp.tril(ones), jnp.tril(ones, k=-1)


def masked_decay(cum, mask):
    diff = cum[..., :, None] - cum[..., None, :]
    return jnp.where(mask, jnp.exp(jnp.where(mask, diff, 0.0)), 0.0)


def inter_chunk_scan(decay, update, axis):
    d = jnp.moveaxis(decay, axis, 0)
    u = jnp.moveaxis(update, axis, 0)

    def step(s, du):
        dn, un = du
        return dn * s + un, s

    _, s_in = lax.scan(step, jnp.zeros_like(u[0]), (d, u))
    return jnp.moveaxis(s_in, 0, axis)


def gla_mixer(q, k, v, lr, r, w_gate2, b_gate2, norm_g):
    incl, _ = chunk_masks()
    gk = jax.nn.log_sigmoid(lr @ w_gate2.astype(jnp.float32) + b_gate2.astype(jnp.float32)) / GLA_GATE_TAU
    q = to_heads(q, GLA_HEADS) * (GLA_DK ** -0.5)
    k = to_heads(k, GLA_HEADS)
    v = to_heads(v, GLA_HEADS)
    gk = to_heads(gk, GLA_HEADS)
    b = jnp.cumsum(gk, axis=3)
    b_last = b[..., -1:, :]
    q_dec = q * jnp.exp(b)
    att = jnp.einsum('bhnld,bhnmd->bhnlm', q_dec, k * jnp.exp(-b))
    o = jnp.einsum('bhnlm,bhnme->bhnle', jnp.where(incl, att, 0.0), v)
    upd = jnp.einsum('bhnld,bhnle->bhnde', k * jnp.exp(b_last - b), v)
    s_in = inter_chunk_scan(jnp.exp(b_last[..., 0, :])[..., None], upd, axis=2)
    o = o + jnp.einsum('bhnld,bhnde->bhnle', q_dec, s_in)
    o = _rms(o) * norm_g.astype(jnp.float32)
    return from_heads(o) * jax.nn.silu(r)


def gdn_mixer(qkv, a_raw, b_raw, g_raw, conv_w, a_log, dt_bias, norm_g):
    incl, strict = chunk_masks()
    qkv = jax.nn.silu(causal_dwconv(qkv, conv_w))
    q, k, v = jnp.split(qkv, [GDN_HEADS * GDN_DK, 2 * GDN_HEADS * GDN_DK], axis=-1)
    q = l2norm(to_heads(q, GDN_HEADS)) * (GDN_DK ** -0.5)
    k = l2norm(to_heads(k, GDN_HEADS))
    v = to_heads(v, GDN_HEADS)
    beta = heads_scalar(jax.nn.sigmoid(b_raw))
    g = heads_scalar(-jnp.exp(a_log.astype(jnp.float32)) * jax.nn.softplus(a_raw + dt_bias.astype(jnp.float32)))
    cum = jnp.cumsum(g, axis=-1)
    gam = masked_decay(cum, incl)
    a_mat = jnp.where(strict, beta[..., None] * jnp.einsum('bhnld,bhnmd->bhnlm', k, k) * gam, 0.0)
    rhs = jnp.concatenate([beta[..., None] * v, (beta * jnp.exp(cum))[..., None] * k], axis=-1)
    sol = lax.linalg.triangular_solve(a_mat + jnp.eye(CHUNK, dtype=jnp.float32), rhs,
                                      left_side=True, lower=True, unit_diagonal=True)
    u_pre, w_mix = sol[..., :GDN_DV], sol[..., GDN_DV:]
    p_mat = jnp.einsum('bhnld,bhnmd->bhnlm', q, k) * gam
    q_dec = q * jnp.exp(cum)[..., None]
    k_dec = k * jnp.exp(cum[..., -1:] - cum)[..., None]
    chunk_decay = jnp.exp(cum[..., -1])

    def step(m, inp):
        u_pre_n, w_n, p_n, q_n, k_n, d_n = inp
        u_n = u_pre_n - jnp.einsum('bhld,bhde->bhle', w_n, m)
        o_n = jnp.einsum('bhld,bhde->bhle', q_n, m) + jnp.einsum('bhlm,bhme->bhle', p_n, u_n)
        m = d_n[..., None, None] * m + jnp.einsum('bhld,bhle->bhde', k_n, u_n)
        return m, o_n

    xs = tuple(jnp.moveaxis(t, 2, 0) for t in (u_pre, w_mix, p_mat, q_dec, k_dec, chunk_decay))
    m0 = jnp.zeros((qkv.shape[0], GDN_HEADS, GDN_DK, GDN_DV), jnp.float32)
    _, o = lax.scan(step, m0, xs)
    o = jnp.moveaxis(o, 0, 2)
    o = _rms(o) * norm_g.astype(jnp.float32)
    return from_heads(o) * jax.nn.silu(g_raw)


def ssd_mixer(z, xbc, dt_raw, conv_w, conv_b, a_log, dt_bias, d_skip, norm_g):
    incl, _ = chunk_masks()
    bsz, seq, _ = z.shape
    nc = seq // CHUNK
    hg = SSD_HEADS // SSD_GROUPS
    xbc = jax.nn.silu(causal_dwconv(xbc, conv_w, conv_b))
    xs, bm, cm = jnp.split(xbc, [SSD_INNER, SSD_INNER + SSD_GROUPS * SSD_STATE], axis=-1)
    x = xs.reshape(bsz, nc, CHUNK, SSD_GROUPS, hg, SSD_HEAD_DIM)
    bm = bm.reshape(bsz, nc, CHUNK, SSD_GROUPS, SSD_STATE)
    cm = cm.reshape(bsz, nc, CHUNK, SSD_GROUPS, SSD_STATE)
    dt = jax.nn.softplus(dt_raw + dt_bias.astype(jnp.float32)).reshape(bsz, nc, CHUNK, SSD_GROUPS, hg)
    a_head = -jnp.exp(a_log.astype(jnp.float32)).reshape(SSD_GROUPS, hg)
    cum = jnp.cumsum((dt * a_head).transpose(0, 1, 3, 4, 2), axis=-1)
    seg = masked_decay(cum, incl)
    xdt = x * dt[..., None]
    cb = jnp.einsum('bclgn,bcsgn->bcgls', cm, bm)
    y = jnp.einsum('bcgls,bcghls,bcsghp->bclghp', cb, seg, xdt)
    states = jnp.einsum('bcsgn,bcghs,bcsghp->bcghnp', bm, jnp.exp(cum[..., -1:] - cum), xdt)
    s_in = inter_chunk_scan(jnp.exp(cum[..., -1])[..., None, None], states, axis=1)
    y = y + jnp.einsum('bclgn,bcghnp,bcghl->bclghp', cm, s_in, jnp.exp(cum))
    y = y + d_skip.astype(jnp.float32).reshape(SSD_GROUPS, hg)[:, :, None] * x
    y = y.reshape(bsz, seq, SSD_INNER) * jax.nn.silu(z)
    y = _rms(y.reshape(bsz, seq, SSD_GROUPS, SSD_INNER // SSD_GROUPS)).reshape(bsz, seq, SSD_INNER)
    return y * norm_g.astype(jnp.float32)


def hybrid_mixer(h, w_in, gla_w_gate2, gla_b_gate2, gla_norm, gdn_conv_w, gdn_a_log, gdn_dt_bias, gdn_norm,
                 ssd_conv_w, ssd_conv_b, ssd_a_log, ssd_dt_bias, ssd_d, ssd_norm,
                 w_branch_gla, w_branch_gdn, w_branch_ssd, b_merge, w_out):
    proj = (h @ w_in).astype(jnp.float32)
    (gla_q, gla_k, gla_v, gla_lr, gla_r, gdn_qkv, gdn_a, gdn_b, gdn_g,
     ssd_z, ssd_xbc, ssd_dt, merge_raw) = jnp.split(proj, _split_points(), axis=-1)
    y_gla = gla_mixer(gla_q, gla_k, gla_v, gla_lr, gla_r, gla_w_gate2, gla_b_gate2, gla_norm).astype(h.dtype) @ w_branch_gla
    y_gdn = gdn_mixer(gdn_qkv, gdn_a, gdn_b, gdn_g, gdn_conv_w, gdn_a_log, gdn_dt_bias, gdn_norm).astype(h.dtype) @ w_branch_gdn
    y_ssd = ssd_mixer(ssd_z, ssd_xbc, ssd_dt, ssd_conv_w, ssd_conv_b, ssd_a_log, ssd_dt_bias, ssd_d, ssd_norm).astype(h.dtype) @ w_branch_ssd
    g_gla, g_gdn, g_ssd = jnp.split(jax.nn.sigmoid(merge_raw + b_merge.astype(jnp.float32)), N_BRANCH, axis=-1)
    merged = g_gla * y_gla + g_gdn * y_gdn + g_ssd * y_ssd
    return merged.astype(h.dtype) @ w_out


def moe_ffn(h, w_router, b_router, w_gate_up, b_gate_up, w_down, b_down):
    bsz, seq, dm = h.shape
    t = h.reshape(-1, dm)
    n_tok = t.shape[0]
    logits = (t @ w_router).astype(jnp.float32) + b_router.astype(jnp.float32)
    top_val, top_idx = lax.top_k(logits, TOP_K)
    gates = jax.nn.softmax(top_val, axis=-1)
    n_assign = n_tok * TOP_K
    flat_e = top_idx.reshape(-1)
    flat_tok = jnp.arange(n_assign, dtype=jnp.int32) // TOP_K
    order = jnp.argsort(flat_e)
    se = flat_e[order]
    counts = jnp.bincount(flat_e, length=N_EXPERTS)
    padded = ((counts + MOE_BLOCK - 1) // MOE_BLOCK) * MOE_BLOCK
    pad_end = jnp.cumsum(padded)
    pad_start = pad_end - padded
    start = jnp.cumsum(counts) - counts
    pos = jnp.arange(n_assign, dtype=jnp.int32) - start[se] + pad_start[se]
    n_blocks = -(-n_assign // MOE_BLOCK) + N_EXPERTS
    n_rows = n_blocks * MOE_BLOCK
    row_tok = jnp.zeros((n_rows,), jnp.int32).at[pos].set(flat_tok[order])
    row_gate = jnp.zeros((n_rows,), jnp.float32).at[pos].set(gates.reshape(-1)[order])
    block_e = jnp.minimum(jnp.searchsorted(pad_end, jnp.arange(n_blocks) * MOE_BLOCK, side='right'), N_EXPERTS - 1)
    xs = t[row_tok].reshape(n_blocks, MOE_BLOCK, dm)

    def expert_block(args):
        xb, e = args
        gu = xb @ w_gate_up[e] + b_gate_up[e]
        gate = jnp.minimum(gu[..., 0::2], SWIGLU_LIMIT)
        up = jnp.clip(gu[..., 1::2], -SWIGLU_LIMIT, SWIGLU_LIMIT)
        act = (up + 1.0) * gate * jax.nn.sigmoid(SWIGLU_ALPHA * gate)
        return act @ w_down[e] + b_down[e]

    ys = lax.map(expert_block, (xs, block_e)).reshape(n_rows, dm)
    y = jax.ops.segment_sum(ys.astype(jnp.float32) * row_gate[:, None], row_tok, num_segments=n_tok)
    return y.reshape(bsz, seq, dm).astype(h.dtype)


def setup_inputs(seed: int = 0) -> dict:
    key = jax.random.key(seed)
    keys = iter(jax.random.split(key, 48))

    def nrm(shape, scale):
        return jax.random.normal(next(keys), shape, jnp.float32) * scale

    def gain(shape):
        return 1.0 + nrm(shape, 0.02)

    def dt_bias_init(n):
        lo, hi = np.log(1e-3), np.log(1e-1)
        dt = jnp.exp(jax.random.uniform(next(keys), (DEPTH, n), jnp.float32, minval=lo, maxval=hi))
        return dt + jnp.log(-jnp.expm1(-dt))

    def a_log_init(n):
        return jnp.log(jax.random.uniform(next(keys), (DEPTH, n), jnp.float32, minval=1.0, maxval=16.0))

    gdn_ch = GDN_HEADS * (2 * GDN_DK + GDN_DV)
    ssd_ch = SSD_INNER + 2 * SSD_GROUPS * SSD_STATE
    return {
        'x': nrm((BATCH, SEQ, D_MODEL), 1.0),
        'c': nrm((BATCH, D_MODEL), 1.0),
        'w_mod': nrm((DEPTH, D_MODEL, 6 * D_MODEL), 0.5 * D_MODEL ** -0.5),
        'b_mod': nrm((DEPTH, 6 * D_MODEL), 0.01),
        'norm_mix': gain((DEPTH, D_MODEL)),
        'norm_ffn': gain((DEPTH, D_MODEL)),
        'norm_final': gain((D_MODEL,)),
        'w_in': nrm((DEPTH, D_MODEL, IN_COLS), D_MODEL ** -0.5),
        'gla_w_gate2': nrm((DEPTH, GLA_GATE_RANK, GLA_HEADS * GLA_DK), GLA_GATE_RANK ** -0.5),
        'gla_b_gate2': nrm((DEPTH, GLA_HEADS * GLA_DK), 0.1),
        'gla_norm': gain((DEPTH, GLA_DV)),
        'gdn_conv_w': nrm((DEPTH, GDN_CONV, gdn_ch), GDN_CONV ** -0.5),
        'gdn_a_log': a_log_init(GDN_HEADS),
        'gdn_dt_bias': dt_bias_init(GDN_HEADS),
        'gdn_norm': gain((DEPTH, GDN_DV)),
        'ssd_conv_w': nrm((DEPTH, SSD_CONV, ssd_ch), SSD_CONV ** -0.5),
        'ssd_conv_b': nrm((DEPTH, ssd_ch), 0.01),
        'ssd_a_log': a_log_init(SSD_HEADS),
        'ssd_dt_bias': dt_bias_init(SSD_HEADS),
        'ssd_d': 1.0 + nrm((DEPTH, SSD_HEADS), 0.1),
        'ssd_norm': gain((DEPTH, SSD_INNER)),
        'w_branch_gla': nrm((DEPTH, GLA_HEADS * GLA_DV, D_MODEL), (GLA_HEADS * GLA_DV) ** -0.5),
        'w_branch_gdn': nrm((DEPTH, GDN_HEADS * GDN_DV, D_MODEL), (GDN_HEADS * GDN_DV) ** -0.5),
        'w_branch_ssd': nrm((DEPTH, SSD_INNER, D_MODEL), SSD_INNER ** -0.5),
        'b_merge': nrm((DEPTH, N_BRANCH * D_MODEL), 0.01),
        'w_out': nrm((DEPTH, D_MODEL, D_MODEL), D_MODEL ** -0.5),
        'w_router': nrm((DEPTH, D_MODEL, N_EXPERTS), D_MODEL ** -0.5),
        'b_router': nrm((DEPTH, N_EXPERTS), 0.01),
        'w_gate_up': nrm((DEPTH, N_EXPERTS, D_MODEL, 2 * D_FF_EXPERT), D_MODEL ** -0.5),
        'b_gate_up': nrm((DEPTH, N_EXPERTS, 2 * D_FF_EXPERT), 0.01),
        'w_down': nrm((DEPTH, N_EXPERTS, D_FF_EXPERT, D_MODEL), D_FF_EXPERT ** -0.5),
        'b_down': nrm((DEPTH, N_EXPERTS, D_MODEL), 0.01),
    }


def reference(x, c, w_mod, b_mod, norm_mix, norm_ffn, norm_final, w_in, gla_w_gate2, gla_b_gate2, gla_norm,
              gdn_conv_w, gdn_a_log, gdn_dt_bias, gdn_norm, ssd_conv_w, ssd_conv_b, ssd_a_log, ssd_dt_bias,
              ssd_d, ssd_norm, w_branch_gla, w_branch_gdn, w_branch_ssd, b_merge, w_out,
              w_router, b_router, w_gate_up, b_gate_up, w_down, b_down):
    c_act = jax.nn.silu(c)
    for l in range(DEPTH):
        mod = (c_act @ w_mod[l] + b_mod[l])[:, None, :]
        sh_m, sc_m, g_m, sh_f, sc_f, g_f = jnp.split(mod, 6, axis=-1)
        h = rmsnorm(x, norm_mix[l]) * (1.0 + sc_m) + sh_m
        mix = hybrid_mixer(h, w_in[l], gla_w_gate2[l], gla_b_gate2[l], gla_norm[l],
                           gdn_conv_w[l], gdn_a_log[l], gdn_dt_bias[l], gdn_norm[l],
                           ssd_conv_w[l], ssd_conv_b[l], ssd_a_log[l], ssd_dt_bias[l], ssd_d[l], ssd_norm[l],
                           w_branch_gla[l], w_branch_gdn[l], w_branch_ssd[l], b_merge[l], w_out[l])
        x = x + (g_m * mix).astype(x.dtype)
        h = rmsnorm(x, norm_ffn[l]) * (1.0 + sc_f) + sh_f
        ffn = moe_ffn(h, w_router[l], b_router[l], w_gate_up[l], b_gate_up[l], w_down[l], b_down[l])
        x = x + (g_f * ffn).astype(x.dtype)
    return rmsnorm(x, norm_final)
```

```python
import functools

import jax
import jax.numpy as jnp
from jax import lax
from jax.experimental import pallas as pl
from jax.experimental.pallas import tpu as pltpu

F32 = jnp.float32
BF16 = jnp.bfloat16
I32 = jnp.int32
HIGHEST = lax.Precision.HIGHEST

CHUNK = 64
NORM_EPS = 1e-6
GLA_HEADS, GLA_DK, GLA_DV, GLA_GATE_RANK, GLA_GATE_TAU = 4, 64, 128, 16, 16.0
GDN_HEADS, GDN_DK, GDN_DV, GDN_CONV = 4, 128, 128, 4
SSD_HEADS, SSD_HEAD_DIM, SSD_GROUPS, SSD_STATE, SSD_CONV = 8, 64, 2, 64, 4
SSD_INNER = SSD_HEADS * SSD_HEAD_DIM
N_BRANCH = 3
N_EXPERTS, TOP_K = 32, 4
SWIGLU_LIMIT, SWIGLU_ALPHA = 7.0, 1.702
MOE_BLOCK = 256

LANES = 128
CONV_PAD = 8
GATE_ROWS = 8
VMEM_LIMIT = 48 * 1024 * 1024


def _params(*sem):
    return pltpu.CompilerParams(dimension_semantics=sem, vmem_limit_bytes=VMEM_LIMIT)


def _mm(a, b):
    return lax.dot_general(a.astype(BF16), b.astype(BF16), (((1,), (0,)), ((), ())),
                           preferred_element_type=F32)


def _mm_nt(a, b):
    return lax.dot_general(a.astype(BF16), b.astype(BF16), (((1,), (1,)), ((), ())),
                           preferred_element_type=F32)


def _mm_tn(a, b):
    return lax.dot_general(a.astype(BF16), b.astype(BF16), (((0,), (0,)), ((), ())),
                           preferred_element_type=F32)


def _mm_f32(a, b):
    return lax.dot_general(a, b, (((1,), (0,)), ((), ())), precision=HIGHEST,
                           preferred_element_type=F32)


def _mm_nt_f32(a, b):
    return lax.dot_general(a, b, (((1,), (1,)), ((), ())), precision=HIGHEST,
                           preferred_element_type=F32)


def _sigmoid(x):
    return 1.0 / (1.0 + jnp.exp(-x))


def _silu(x):
    return x * _sigmoid(x)


def _softplus(x):
    return jnp.maximum(x, 0.0) + jnp.log1p(jnp.exp(-jnp.abs(x)))


def _log_sigmoid(x):
    return jnp.minimum(x, 0.0) - jnp.log1p(jnp.exp(-jnp.abs(x)))


def _rms(x):
    return x * lax.rsqrt(jnp.mean(x * x, axis=-1, keepdims=True) + NORM_EPS)


def _norm_mod(x, gain, scale, shift):
    return _rms(x) * gain * (1.0 + scale) + shift


def _chunk_masks():
    row = lax.broadcasted_iota(I32, (CHUNK, CHUNK), 0)
    col = lax.broadcasted_iota(I32, (CHUNK, CHUNK), 1)
    return row >= col, row > col, row <= col


def _masked_decay(cum_col, cum_row, mask):
    return jnp.where(mask, jnp.exp(jnp.where(mask, cum_col - cum_row, 0.0)), 0.0)


def _mod_kernel(c_ref, w_ref, b_ref, o_ref):
    o_ref[...] = _mm_f32(_silu(c_ref[...]), w_ref[...]) + b_ref[...]


def _modulation(c, w_mod, b_mod):
    depth, d, six_d = w_mod.shape
    bsz = c.shape[0]
    n_col = six_d // d
    return pl.pallas_call(
        _mod_kernel,
        out_shape=jax.ShapeDtypeStruct((depth, bsz, six_d), F32),
        grid=(depth, n_col),
        in_specs=[
            pl.BlockSpec((bsz, d), lambda l, j: (0, 0)),
            pl.BlockSpec((None, d, d), lambda l, j: (l, 0, j)),
            pl.BlockSpec((None, 1, d), lambda l, j: (l, 0, j)),
        ],
        out_specs=pl.BlockSpec((None, bsz, d), lambda l, j: (l, 0, j)),
        compiler_params=_params("arbitrary", "arbitrary"),
        name="adaln_mod",
    )(c, w_mod, b_mod.reshape(depth, 1, six_d))


def _inproj_kernel(x_ref, sc_ref, sh_ref, g_ref, w_ref, *rest, has_t):
    h = _norm_mod(x_ref[...], g_ref[...], sc_ref[...], sh_ref[...]).astype(BF16)
    if has_t:
        wt_ref, o_ref, ot_ref = rest
        ot_ref[...] = _mm_nt(wt_ref[...], h)
    else:
        (o_ref,) = rest
    o_ref[...] = _mm(h, w_ref[...])


def _inproj(x2, scale, shift, gain, w, wt, *, seq, tm):
    n, d = x2.shape
    width = w.shape[1]
    per_batch = seq // tm
    batch_spec = pl.BlockSpec((None, 1, d), lambda i: (i // per_batch, 0, 0))
    in_specs = [
        pl.BlockSpec((tm, d), lambda i: (i, 0)),
        batch_spec, batch_spec,
        pl.BlockSpec((1, d), lambda i: (0, 0)),
        pl.BlockSpec((d, width), lambda i: (0, 0)),
    ]
    args = [x2, scale, shift, gain, w]
    out_shape = [jax.ShapeDtypeStruct((n, width), F32)]
    out_specs = [pl.BlockSpec((tm, width), lambda i: (i, 0))]
    if wt is not None:
        in_specs.append(pl.BlockSpec((GATE_ROWS, d), lambda i: (0, 0)))
        args.append(wt)
        out_shape.append(jax.ShapeDtypeStruct((GATE_ROWS, n), F32))
        out_specs.append(pl.BlockSpec((GATE_ROWS, tm), lambda i: (0, i)))
    outs = pl.pallas_call(
        functools.partial(_inproj_kernel, has_t=wt is not None),
        out_shape=out_shape, grid=(n // tm,), in_specs=in_specs, out_specs=out_specs,
        compiler_params=_params("arbitrary"),
        name="inproj",
    )(*args)
    return outs if wt is not None else outs[0]


GLA_QK = GLA_HEADS * GLA_DK
GLA_V = GLA_HEADS * GLA_DV
GLA_WIDTH = 2 * GLA_QK + 2 * GLA_V + LANES


def _gla_kernel(p_ref, w2_ref, b2_ref, ng_ref, o_ref, st_ref):
    @pl.when(pl.program_id(1) == 0)
    def _():
        st_ref[...] = jnp.zeros_like(st_ref)

    incl, _, _ = _chunk_masks()
    q = p_ref[:, 0:GLA_QK] * (GLA_DK ** -0.5)
    k = p_ref[:, GLA_QK:2 * GLA_QK]
    lr = p_ref[:, 2 * GLA_QK + 2 * GLA_V:GLA_WIDTH]
    gk = _log_sigmoid(_mm(lr, w2_ref[...]) + b2_ref[...]) / GLA_GATE_TAU
    bcum = _mm_f32(jnp.where(incl, 1.0, 0.0), gk)
    b_last = bcum[CHUNK - 1:CHUNK, :]
    q_dec = q * jnp.exp(bcum)
    k_neg = k * jnp.exp(-bcum)
    k_pos = k * jnp.exp(b_last - bcum)
    st_decay = jnp.exp(b_last)
    for h in range(GLA_HEADS):
        ks = slice(h * GLA_DK, (h + 1) * GLA_DK)
        vs = slice(2 * GLA_QK + h * GLA_DV, 2 * GLA_QK + (h + 1) * GLA_DV)
        rs = slice(2 * GLA_QK + GLA_V + h * GLA_DV, 2 * GLA_QK + GLA_V + (h + 1) * GLA_DV)
        v = p_ref[:, vs]
        st = st_ref[h]
        att = jnp.where(incl, _mm_nt(q_dec[:, ks], k_neg[:, ks]), 0.0)
        o = _mm(att, v) + _mm_nt(q_dec[:, ks], st)
        st_ref[h] = st * st_decay[:, ks] + _mm_tn(v, k_pos[:, ks])
        o = _rms(o) * ng_ref[...]
        o_ref[:, h * GLA_DV:(h + 1) * GLA_DV] = (o * _silu(p_ref[:, rs])).astype(o_ref.dtype)


def _gla(p, w2, b2, ng, *, bsz, seq):
    nc = seq // CHUNK
    return pl.pallas_call(
        _gla_kernel,
        out_shape=jax.ShapeDtypeStruct((bsz * seq, GLA_V), BF16),
        grid=(bsz, nc),
        in_specs=[
            pl.BlockSpec((CHUNK, GLA_WIDTH), lambda b, c: (b * nc + c, 0)),
            pl.BlockSpec((LANES, GLA_QK), lambda b, c: (0, 0)),
            pl.BlockSpec((1, GLA_QK), lambda b, c: (0, 0)),
            pl.BlockSpec((1, GLA_DV), lambda b, c: (0, 0)),
        ],
        out_specs=pl.BlockSpec((CHUNK, GLA_V), lambda b, c: (b * nc + c, 0)),
        scratch_shapes=[pltpu.VMEM((GLA_HEADS, GLA_DV, GLA_DK), F32)],
        compiler_params=_params("arbitrary", "arbitrary"),
        name="gla_mixer",
    )(p, w2, b2, ng)


def _causal_conv(u_ref, ext_ref, w_ref, taps):
    ext_ref[CONV_PAD:CONV_PAD + CHUNK, :] = u_ref[...]
    out = None
    for j in range(taps):
        start = CONV_PAD - (taps - 1) + j
        term = ext_ref[start:start + CHUNK, :] * w_ref[j:j + 1, :]
        out = term if out is None else out + term
    ext_ref[0:CONV_PAD, :] = ext_ref[CHUNK:CHUNK + CONV_PAD, :]
    return out


SSD_BC = SSD_GROUPS * SSD_STATE
SSD_XBC = SSD_INNER + 2 * SSD_BC
SSD_WIDTH = SSD_INNER + SSD_XBC + LANES
SSD_HG = SSD_HEADS // SSD_GROUPS


def _ssd_kernel(p_ref, dtt_ref, cw_ref, cb_ref, alog_ref, dtb_ref, alog_c_ref, dtb_c_ref,
                dskip_ref, ng_ref, o_ref, ext_ref, st_ref, y_ref):
    @pl.when(pl.program_id(1) == 0)
    def _():
        st_ref[...] = jnp.zeros_like(st_ref)
        ext_ref[0:CONV_PAD, :] = jnp.zeros((CONV_PAD, SSD_XBC), F32)

    incl, _, upper = _chunk_masks()
    xbc = _silu(_causal_conv(p_ref.at[:, SSD_INNER:SSD_INNER + SSD_XBC], ext_ref, cw_ref, SSD_CONV)
                + cb_ref[...])
    xs = xbc[:, 0:SSD_INNER]
    dt_col = _softplus(p_ref[:, SSD_INNER + SSD_XBC:SSD_WIDTH] + dtb_ref[...])
    cum_col = _mm_f32(jnp.where(incl, 1.0, 0.0), dt_col * (-jnp.exp(alog_ref[...])))
    dt_row = _softplus(dtt_ref[...] + dtb_c_ref[...])
    cum_row = _mm_f32(dt_row * (-jnp.exp(alog_c_ref[...])), jnp.where(upper, 1.0, 0.0))
    cum_last = cum_col[CHUNK - 1:CHUNK, :]
    for g in range(SSD_GROUPS):
        bm = xbc[:, SSD_INNER + g * SSD_STATE:SSD_INNER + (g + 1) * SSD_STATE]
        cm = xbc[:, SSD_INNER + SSD_BC + g * SSD_STATE:SSD_INNER + SSD_BC + (g + 1) * SSD_STATE]
        cb = _mm_nt(cm, bm)
        for hh in range(SSD_HG):
            h = g * SSD_HG + hh
            hs = slice(h * SSD_HEAD_DIM, (h + 1) * SSD_HEAD_DIM)
            cc = cum_col[:, h:h + 1]
            seg = _masked_decay(cc, cum_row[h:h + 1, :], incl)
            xdt = xs[:, hs] * dt_col[:, h:h + 1]
            st = st_ref[h]
            y = _mm(cb * seg, xdt) + _mm(cm * jnp.exp(cc), st)
            cl = cum_last[:, h:h + 1]
            st_ref[h] = jnp.exp(cl) * st + _mm_tn(bm * jnp.exp(cl - cc), xdt)
            y_ref[:, hs] = y
    y = (y_ref[...] + dskip_ref[...] * xs) * _silu(p_ref[:, 0:SSD_INNER])
    gw = SSD_INNER // SSD_GROUPS
    for g in range(SSD_GROUPS):
        gs = slice(g * gw, (g + 1) * gw)
        o_ref[:, gs] = (_rms(y[:, gs]) * ng_ref[:, gs]).astype(o_ref.dtype)


def _ssd(p, dtt, conv_w, conv_b, a_log, dt_bias, d_skip, ng, *, bsz, seq):
    nc = seq // CHUNK
    pad = lambda v: jnp.pad(v, (0, LANES - v.shape[0])).reshape(1, LANES)
    col = lambda v: v.reshape(SSD_HEADS, 1)
    const = lambda shape: pl.BlockSpec(shape, lambda b, c: (0,) * len(shape))
    return pl.pallas_call(
        _ssd_kernel,
        out_shape=jax.ShapeDtypeStruct((bsz * seq, SSD_INNER), BF16),
        grid=(bsz, nc),
        in_specs=[
            pl.BlockSpec((CHUNK, SSD_WIDTH), lambda b, c: (b * nc + c, 0)),
            pl.BlockSpec((None, GATE_ROWS, CHUNK), lambda b, c: (b * nc + c, 0, 0)),
            const((SSD_CONV, SSD_XBC)), const((1, SSD_XBC)),
            const((1, LANES)), const((1, LANES)),
            const((SSD_HEADS, 1)), const((SSD_HEADS, 1)),
            const((1, SSD_INNER)), const((1, SSD_INNER)),
        ],
        out_specs=pl.BlockSpec((CHUNK, SSD_INNER), lambda b, c: (b * nc + c, 0)),
        scratch_shapes=[
            pltpu.VMEM((CONV_PAD + CHUNK, SSD_XBC), F32),
            pltpu.VMEM((SSD_HEADS, SSD_STATE, SSD_HEAD_DIM), F32),
            pltpu.VMEM((CHUNK, SSD_INNER), F32),
        ],
        compiler_params=_params("arbitrary", "arbitrary"),
        name="ssd_mixer",
    )(p, dtt, conv_w, conv_b.reshape(1, SSD_XBC), pad(a_log), pad(dt_bias), col(a_log), col(dt_bias),
      jnp.repeat(d_skip, SSD_HEAD_DIM).reshape(1, SSD_INNER), ng.reshape(1, SSD_INNER))


GDN_QK = GDN_HEADS * GDN_DK
GDN_V = GDN_HEADS * GDN_DV
GDN_QKV = 2 * GDN_QK + GDN_V
GDN_WIDTH = GDN_QKV + GDN_V + LANES


def _l2norm(x):
    return x * lax.rsqrt(jnp.sum(x * x, axis=-1, keepdims=True) + 1e-6)


def _unit_lower_inverse(a):
    row = lax.broadcasted_iota(I32, a.shape, 0)
    col = lax.broadcasted_iota(I32, a.shape, 1)
    n = -a
    p = jnp.where(row == col, 1.0, 0.0) + n
    steps = (a.shape[0] - 1).bit_length() - 1
    for _ in range(steps):
        n = _mm_f32(n, n)
        p = p + _mm_f32(p, n)
    return p


def _gdn_kernel(p_ref, abt_ref, cw_ref, alog_ref, dtb_ref, alog_c_ref, dtb_c_ref, ng_ref,
                o_ref, ext_ref, st_ref):
    @pl.when(pl.program_id(1) == 0)
    def _():
        st_ref[...] = jnp.zeros_like(st_ref)
        ext_ref[0:CONV_PAD, :] = jnp.zeros((CONV_PAD, GDN_QKV), F32)

    incl, strict, upper = _chunk_masks()
    qkv = _silu(_causal_conv(p_ref.at[:, 0:GDN_QKV], ext_ref, cw_ref, GDN_CONV))
    ab = p_ref[:, GDN_QKV + GDN_V:GDN_WIDTH]
    beta_all = _sigmoid(ab)
    g_col = -jnp.exp(alog_ref[...]) * _softplus(ab + dtb_ref[...])
    cum_col = _mm_f32(jnp.where(incl, 1.0, 0.0), g_col)
    g_row = -jnp.exp(alog_c_ref[...]) * _softplus(abt_ref[...] + dtb_c_ref[...])
    cum_row = _mm_f32(g_row, jnp.where(upper, 1.0, 0.0))
    cum_last = cum_col[CHUNK - 1:CHUNK, :]
    for h in range(GDN_HEADS):
        q = _l2norm(qkv[:, h * GDN_DK:(h + 1) * GDN_DK]) * (GDN_DK ** -0.5)
        k = _l2norm(qkv[:, GDN_QK + h * GDN_DK:GDN_QK + (h + 1) * GDN_DK])
        v = qkv[:, 2 * GDN_QK + h * GDN_DV:2 * GDN_QK + (h + 1) * GDN_DV]
        beta = beta_all[:, GDN_HEADS + h:GDN_HEADS + h + 1]
        cc = cum_col[:, h:h + 1]
        cl = cum_last[:, h:h + 1]
        gam = _masked_decay(cc, cum_row[h:h + 1, :], incl)
        a_mat = jnp.where(strict, beta * _mm_nt(k, k) * gam, 0.0)
        t_inv = _unit_lower_inverse(a_mat)
        u_pre = _mm_f32(t_inv, beta * v)
        w_mix = _mm_f32(t_inv, (beta * jnp.exp(cc)) * k)
        p_mat = _mm_nt(q, k) * gam
        q_dec = q * jnp.exp(cc)
        k_dec = k * jnp.exp(cl - cc)
        m = st_ref[h]
        u = u_pre - _mm(w_mix, m)
        o = _mm(q_dec, m) + _mm(p_mat, u)
        st_ref[h] = jnp.exp(cl) * m + _mm_tn(k_dec, u)
        o = _rms(o) * ng_ref[...]
        gs = slice(GDN_QKV + h * GDN_DV, GDN_QKV + (h + 1) * GDN_DV)
        o_ref[:, h * GDN_DV:(h + 1) * GDN_DV] = (o * _silu(p_ref[:, gs])).astype(o_ref.dtype)


def _gdn(p, abt, conv_w, a_log, dt_bias, ng, *, bsz, seq):
    nc = seq // CHUNK
    pad = lambda v: jnp.pad(v, (0, LANES - v.shape[0])).reshape(1, LANES)
    col = lambda v: jnp.pad(v, (0, GATE_ROWS - v.shape[0])).reshape(GATE_ROWS, 1)
    const = lambda shape: pl.BlockSpec(shape, lambda b, c: (0,) * len(shape))
    return pl.pallas_call(
        _gdn_kernel,
        out_shape=jax.ShapeDtypeStruct((bsz * seq, GDN_V), BF16),
        grid=(bsz, nc),
        in_specs=[
            pl.BlockSpec((CHUNK, GDN_WIDTH), lambda b, c: (b * nc + c, 0)),
            pl.BlockSpec((None, GATE_ROWS, CHUNK), lambda b, c: (b * nc + c, 0, 0)),
            const((GDN_CONV, GDN_QKV)),
            const((1, LANES)), const((1, LANES)),
            const((GATE_ROWS, 1)), const((GATE_ROWS, 1)),
            const((1, GDN_DV)),
        ],
        out_specs=pl.BlockSpec((CHUNK, GDN_V), lambda b, c: (b * nc + c, 0)),
        scratch_shapes=[
            pltpu.VMEM((CONV_PAD + CHUNK, GDN_QKV), F32),
            pltpu.VMEM((GDN_HEADS, GDN_DK, GDN_DV), F32),
        ],
        compiler_params=_params("arbitrary", "arbitrary"),
        name="gdn_mixer",
    )(p, abt, conv_w, pad(a_log), pad(dt_bias), col(a_log), col(dt_bias), ng.reshape(1, GDN_DV))


def _merge_kernel(x_ref, ya_ref, yb_ref, yc_ref, mr_ref, bm_ref, wa_ref, wb_ref, wc_ref, wo_ref,
                  gm_ref, o_ref):
    d = x_ref.shape[1]
    merged = None
    for i, (y_ref, w_ref) in enumerate(((ya_ref, wa_ref), (yb_ref, wb_ref), (yc_ref, wc_ref))):
        gate = _sigmoid(mr_ref[:, i * d:(i + 1) * d] + bm_ref[:, i * d:(i + 1) * d])
        term = gate * _mm(y_ref[...], w_ref[...])
        merged = term if merged is None else merged + term
    o_ref[...] = x_ref[...] + gm_ref[...] * _mm(merged, wo_ref[...])


def _merge(x2, ya, yb, yc, mr, b_merge, wa, wb, wc, wo, gm, *, seq, tm):
    n, d = x2.shape
    per_batch = seq // tm
    row = lambda w: pl.BlockSpec((tm, w), lambda i: (i, 0))
    const = lambda shape: pl.BlockSpec(shape, lambda i: (0,) * len(shape))
    return pl.pallas_call(
        _merge_kernel,
        out_shape=jax.ShapeDtypeStruct((n, d), F32),
        grid=(n // tm,),
        in_specs=[
            row(d), row(ya.shape[1]), row(yb.shape[1]), row(yc.shape[1]), row(N_BRANCH * d),
            const((1, N_BRANCH * d)),
            const(wa.shape), const(wb.shape), const(wc.shape), const(wo.shape),
            pl.BlockSpec((None, 1, d), lambda i: (i // per_batch, 0, 0)),
        ],
        out_specs=row(d),
        compiler_params=_params("arbitrary"),
        name="merge_out",
    )(x2, ya, yb, yc, mr, b_merge.reshape(1, N_BRANCH * d), wa, wb, wc, wo, gm)


def _router_kernel(x_ref, sc_ref, sh_ref, g_ref, wr_ref, br_ref,
                   h_ref, idx_ref, gate_ref, rank_ref, cnt_ref, carry_ref):
    @pl.when(pl.program_id(0) == 0)
    def _():
        carry_ref[...] = jnp.zeros_like(carry_ref)

    tm = x_ref.shape[0]
    h = _norm_mod(x_ref[...], g_ref[...], sc_ref[...], sh_ref[...])
    h_ref[...] = h
    logits = _mm_nt_f32(wr_ref[...], h) + br_ref[...]
    e_iota = lax.broadcasted_iota(I32, logits.shape, 0).astype(F32)
    vals, idxs, sels = [], [], []
    cur = logits
    for _ in range(TOP_K):
        m = jnp.max(cur, axis=0, keepdims=True)
        idx = jnp.min(jnp.where(cur == m, e_iota, float(N_EXPERTS)), axis=0, keepdims=True)
        sel = e_iota == idx
        cur = jnp.where(sel, -jnp.inf, cur)
        vals.append(m)
        idxs.append(idx.astype(I32))
        sels.append(sel)
    exps = [jnp.exp(v - vals[0]) for v in vals]
    denom = exps[0] + exps[1] + exps[2] + exps[3]
    onehot = jnp.zeros(logits.shape, F32)
    for sel in sels:
        onehot = onehot + jnp.where(sel, 1.0, 0.0)
    row = lax.broadcasted_iota(I32, (tm, tm), 0)
    col = lax.broadcasted_iota(I32, (tm, tm), 1)
    before = _mm(onehot, jnp.where(row < col, 1.0, 0.0)) + carry_ref[...]
    zeros_i = jnp.zeros((GATE_ROWS - TOP_K, tm), I32)
    idx_ref[...] = jnp.concatenate(idxs + [zeros_i], axis=0)
    gate_ref[...] = jnp.concatenate([e / denom for e in exps] + [zeros_i.astype(F32)], axis=0)
    ranks = [jnp.sum(jnp.where(sel, before, 0.0), axis=0, keepdims=True).astype(I32) for sel in sels]
    rank_ref[...] = jnp.concatenate(ranks + [zeros_i], axis=0)
    carry_ref[...] = carry_ref[...] + jnp.sum(onehot, axis=1, keepdims=True)
    cnt_ref[...] = jnp.broadcast_to(carry_ref[...], cnt_ref.shape).astype(I32)


def _router(x2, scale, shift, gain, w_router_t, b_router, *, seq, tm):
    n, d = x2.shape
    per_batch = seq // tm
    batch_spec = pl.BlockSpec((None, 1, d), lambda i: (i // per_batch, 0, 0))
    lane_out = pl.BlockSpec((GATE_ROWS, tm), lambda i: (0, i))
    return pl.pallas_call(
        _router_kernel,
        out_shape=[
            jax.ShapeDtypeStruct((n, d), F32),
            jax.ShapeDtypeStruct((GATE_ROWS, n), I32),
            jax.ShapeDtypeStruct((GATE_ROWS, n), F32),
            jax.ShapeDtypeStruct((GATE_ROWS, n), I32),
            jax.ShapeDtypeStruct((N_EXPERTS, LANES), I32),
        ],
        grid=(n // tm,),
        in_specs=[
            pl.BlockSpec((tm, d), lambda i: (i, 0)),
            batch_spec, batch_spec,
            pl.BlockSpec((1, d), lambda i: (0, 0)),
            pl.BlockSpec((N_EXPERTS, d), lambda i: (0, 0)),
            pl.BlockSpec((N_EXPERTS, 1), lambda i: (0, 0)),
        ],
        out_specs=[
            pl.BlockSpec((tm, d), lambda i: (i, 0)),
            lane_out, lane_out, lane_out,
            pl.BlockSpec((N_EXPERTS, LANES), lambda i: (0, 0)),
        ],
        scratch_shapes=[pltpu.VMEM((N_EXPERTS, 1), F32)],
        compiler_params=_params("arbitrary"),
        name="router",
    )(x2, scale, shift, gain, w_router_t, b_router.reshape(N_EXPERTS, 1))


def _row_copy(src, src_row, dst, dst_row, sem):
    return pltpu.make_async_copy(src.at[pl.ds(src_row, 1), :], dst.at[pl.ds(dst_row, 1), :], sem)


def _load_positions(pos_hbm, pos_smem, sem, count):
    start = pl.multiple_of(pl.program_id(0) * count, count)
    cp = pltpu.make_async_copy(pos_hbm.at[pl.ds(start, count)], pos_smem, sem)
    cp.start()
    cp.wait()


def _dispatch_kernel(pos_hbm, h_ref, xs_init, xs_hbm, pos_smem, pos_sem, row_sem):
    del xs_init
    td = h_ref.shape[0]
    _load_positions(pos_hbm, pos_smem, pos_sem, TOP_K * td)

    def start(t, carry):
        for k in range(TOP_K):
            _row_copy(h_ref, t, xs_hbm, pos_smem[TOP_K * t + k], row_sem).start()
        return carry

    def wait(t, carry):
        for k in range(TOP_K):
            _row_copy(h_ref, 0, xs_hbm, 0, row_sem).wait()
        return carry

    lax.fori_loop(0, td, start, 0)
    lax.fori_loop(0, td, wait, 0)


def _dispatch(pos_flat, h2, n_rows, *, td):
    n, d = h2.shape
    xs0 = jnp.zeros((n_rows, d), F32)
    return pl.pallas_call(
        _dispatch_kernel,
        out_shape=jax.ShapeDtypeStruct((n_rows, d), F32),
        grid=(n // td,),
        in_specs=[
            pl.BlockSpec(memory_space=pl.ANY),
            pl.BlockSpec((td, d), lambda i: (i, 0)),
            pl.BlockSpec(memory_space=pl.ANY),
        ],
        out_specs=pl.BlockSpec(memory_space=pl.ANY),
        scratch_shapes=[
            pltpu.SMEM((TOP_K * td,), I32),
            pltpu.SemaphoreType.DMA,
            pltpu.SemaphoreType.DMA,
        ],
        input_output_aliases={2: 0},
        compiler_params=_params("arbitrary"),
        name="moe_dispatch",
    )(pos_flat, h2, xs0)


def _expert_kernel(be_ref, x_ref, wgu_ref, bgu_ref, wd_ref, bd_ref, o_ref):
    del be_ref
    f = wd_ref.shape[0]
    gu = _mm(x_ref[...], wgu_ref[...]) + bgu_ref[...]
    gate = jnp.minimum(gu[:, 0:f], SWIGLU_LIMIT)
    up = jnp.clip(gu[:, f:2 * f], -SWIGLU_LIMIT, SWIGLU_LIMIT)
    act = (up + 1.0) * gate * _sigmoid(SWIGLU_ALPHA * gate)
    o_ref[...] = _mm(act, wd_ref[...]) + bd_ref[...]


def _experts(block_e, xs, wgu, bgu, wd, bd):
    n_rows, d = xs.shape
    f = wd.shape[1]
    n_blocks = n_rows // MOE_BLOCK
    grid_spec = pltpu.PrefetchScalarGridSpec(
        num_scalar_prefetch=1,
        grid=(n_blocks,),
        in_specs=[
            pl.BlockSpec((MOE_BLOCK, d), lambda i, be: (i, 0)),
            pl.BlockSpec((None, d, 2 * f), lambda i, be: (be[i], 0, 0)),
            pl.BlockSpec((None, 1, 2 * f), lambda i, be: (be[i], 0, 0)),
            pl.BlockSpec((None, f, d), lambda i, be: (be[i], 0, 0)),
            pl.BlockSpec((None, 1, d), lambda i, be: (be[i], 0, 0)),
        ],
        out_specs=pl.BlockSpec((MOE_BLOCK, d), lambda i, be: (i, 0)),
    )
    return pl.pallas_call(
        _expert_kernel,
        out_shape=jax.ShapeDtypeStruct((n_rows, d), F32),
        grid_spec=grid_spec,
        compiler_params=_params("arbitrary"),
        name="moe_experts",
    )(block_e, xs, wgu, bgu, wd, bd)


def _combine_kernel(pos_hbm, ys_hbm, gate_ref, x_ref, gf_ref, o_ref, buf_ref, pos_smem, pos_sem,
                    row_sem):
    tc = x_ref.shape[0]
    _load_positions(pos_hbm, pos_smem, pos_sem, TOP_K * tc)

    def start(t, carry):
        for k in range(TOP_K):
            _row_copy(ys_hbm, pos_smem[TOP_K * t + k], buf_ref.at[k], t, row_sem).start()
        return carry

    def wait(t, carry):
        for k in range(TOP_K):
            _row_copy(ys_hbm, 0, buf_ref.at[k], 0, row_sem).wait()
        return carry

    lax.fori_loop(0, tc, start, 0)
    lax.fori_loop(0, tc, wait, 0)
    y = None
    for k in range(TOP_K):
        term = gate_ref[:, k:k + 1] * buf_ref[k]
        y = term if y is None else y + term
    o_ref[...] = x_ref[...] + gf_ref[...] * y


def _combine(pos_flat, ys, gates, x2, gf, *, seq, tc):
    n, d = x2.shape
    per_batch = seq // tc
    return pl.pallas_call(
        _combine_kernel,
        out_shape=jax.ShapeDtypeStruct((n, d), F32),
        grid=(n // tc,),
        in_specs=[
            pl.BlockSpec(memory_space=pl.ANY),
            pl.BlockSpec(memory_space=pl.ANY),
            pl.BlockSpec((tc, TOP_K), lambda i: (i, 0)),
            pl.BlockSpec((tc, d), lambda i: (i, 0)),
            pl.BlockSpec((None, 1, d), lambda i: (i // per_batch, 0, 0)),
        ],
        out_specs=pl.BlockSpec((tc, d), lambda i: (i, 0)),
        scratch_shapes=[
            pltpu.VMEM((TOP_K, tc, d), F32),
            pltpu.SMEM((TOP_K * tc,), I32),
            pltpu.SemaphoreType.DMA,
            pltpu.SemaphoreType.DMA,
        ],
        compiler_params=_params("arbitrary"),
        name="moe_combine",
    )(pos_flat, ys, gates, x2, gf)


def _final_norm_kernel(x_ref, g_ref, o_ref):
    o_ref[...] = _rms(x_ref[...]) * g_ref[...]


def _final_norm(x2, gain, *, tm):
    n, d = x2.shape
    return pl.pallas_call(
        _final_norm_kernel,
        out_shape=jax.ShapeDtypeStruct((n, d), F32),
        grid=(n // tm,),
        in_specs=[pl.BlockSpec((tm, d), lambda i: (i, 0)), pl.BlockSpec((1, d), lambda i: (0, 0))],
        out_specs=pl.BlockSpec((tm, d), lambda i: (i, 0)),
        compiler_params=_params("arbitrary"),
        name="final_norm",
    )(x2, gain.reshape(1, d))


def _split_w_in(w_in):
    d = w_in.shape[0]
    sizes = (GLA_QK, GLA_QK, GLA_V, GLA_GATE_RANK, GLA_V, GDN_QKV, GDN_HEADS, GDN_HEADS, GDN_V,
             SSD_INNER, SSD_XBC, SSD_HEADS, N_BRANCH * d)
    parts, acc = [], 0
    for s in sizes:
        parts.append(w_in[:, acc:acc + s])
        acc += s
    (gq, gk, gv, glr, gr, dqkv, da, db, dg, sz, sxbc, sdt, mrg) = parts
    zeros = lambda w: jnp.zeros((d, w), w_in.dtype)
    w_gla = jnp.concatenate([gq, gk, gv, gr, glr, zeros(LANES - GLA_GATE_RANK)], axis=1)
    w_gdn = jnp.concatenate([dqkv, dg, da, db, zeros(LANES - 2 * GDN_HEADS)], axis=1)
    w_ssd = jnp.concatenate([sz, sxbc, sdt, zeros(LANES - SSD_HEADS)], axis=1)
    pad_rows = lambda m: jnp.pad(m, ((0, GATE_ROWS - m.shape[0]), (0, 0)))
    w_gdn_t = pad_rows(jnp.concatenate([da, db], axis=1).T)
    w_ssd_t = pad_rows(sdt.T)
    cast = lambda m: m.astype(BF16)
    return cast(w_gla), cast(w_gdn), cast(w_gdn_t), cast(w_ssd), cast(w_ssd_t), cast(mrg)


def _head_major(gt, n_chunks):
    return gt.reshape(gt.shape[0], n_chunks, CHUNK).transpose(1, 0, 2)


def _moe(x2, scale, shift, gain, gf, w_router, b_router, w_gate_up, b_gate_up, w_down, b_down, *,
         seq, tm):
    n, d = x2.shape
    h2, idx_t, gate_t, rank_t, counts = _router(x2, scale, shift, gain, w_router.T, b_router,
                                                seq=seq, tm=tm)
    counts = counts[:, 0]
    padded = ((counts + MOE_BLOCK - 1) // MOE_BLOCK) * MOE_BLOCK
    pad_end = jnp.cumsum(padded)
    pad_start = pad_end - padded
    n_blocks = -(-(n * TOP_K) // MOE_BLOCK) + N_EXPERTS
    block_e = jnp.minimum(
        jnp.searchsorted(pad_end, jnp.arange(n_blocks, dtype=I32) * MOE_BLOCK, side='right'),
        N_EXPERTS - 1).astype(I32)
    pos = (pad_start[idx_t[:TOP_K]] + rank_t[:TOP_K]).astype(I32)
    pos_flat = pos.T.reshape(-1)
    gates = gate_t[:TOP_K].T
    f = w_down.shape[1]
    wgu = jnp.concatenate([w_gate_up[..., 0::2], w_gate_up[..., 1::2]], axis=-1).astype(BF16)
    bgu = jnp.concatenate([b_gate_up[..., 0::2], b_gate_up[..., 1::2]], axis=-1)
    xs = _dispatch(pos_flat, h2, n_blocks * MOE_BLOCK, td=MOE_BLOCK)
    ys = _experts(block_e, xs, wgu, bgu.reshape(N_EXPERTS, 1, 2 * f), w_down.astype(BF16),
                  b_down.reshape(N_EXPERTS, 1, d))
    return _combine(pos_flat, ys, gates, x2, gf, seq=seq, tc=MOE_BLOCK)


def kernel(x, c, w_mod, b_mod, norm_mix, norm_ffn, norm_final, w_in, gla_w_gate2, gla_b_gate2, gla_norm, gdn_conv_w, gdn_a_log, gdn_dt_bias, gdn_norm, ssd_conv_w, ssd_conv_b, ssd_a_log, ssd_dt_bias, ssd_d, ssd_norm, w_branch_gla, w_branch_gdn, w_branch_ssd, b_merge, w_out, w_router, b_router, w_gate_up, b_gate_up, w_down, b_down):
    bsz, seq, d = x.shape
    depth = w_mod.shape[0]
    n = bsz * seq
    nc = n // CHUNK
    tm = min(512, seq)
    mod = _modulation(c, w_mod, b_mod)
    x2 = x.reshape(n, d)
    for l in range(depth):
        sh_m, sc_m, g_m, sh_f, sc_f, g_f = [mod[l, :, i * d:(i + 1) * d].reshape(bsz, 1, d)
                                            for i in range(6)]
        w_gla, w_gdn, w_gdn_t, w_ssd, w_ssd_t, w_mrg = _split_w_in(w_in[l])
        gain = norm_mix[l].reshape(1, d)
        proj = functools.partial(_inproj, x2, sc_m, sh_m, gain, seq=seq, tm=tm)
        p_gla = proj(w_gla, None)
        p_gdn, abt = proj(w_gdn, w_gdn_t)
        p_ssd, dtt = proj(w_ssd, w_ssd_t)
        p_mrg = proj(w_mrg, None)
        w2 = jnp.pad(gla_w_gate2[l], ((0, LANES - GLA_GATE_RANK), (0, 0))).astype(BF16)
        y_gla = _gla(p_gla, w2, gla_b_gate2[l].reshape(1, GLA_QK), gla_norm[l].reshape(1, GLA_DV),
                     bsz=bsz, seq=seq)
        y_gdn = _gdn(p_gdn, _head_major(abt, nc), gdn_conv_w[l], gdn_a_log[l], gdn_dt_bias[l],
                     gdn_norm[l], bsz=bsz, seq=seq)
        y_ssd = _ssd(p_ssd, _head_major(dtt, nc), ssd_conv_w[l], ssd_conv_b[l], ssd_a_log[l],
                     ssd_dt_bias[l], ssd_d[l], ssd_norm[l], bsz=bsz, seq=seq)
        x2 = _merge(x2, y_gla, y_gdn, y_ssd, p_mrg, b_merge[l],
                    w_branch_gla[l].astype(BF16), w_branch_gdn[l].astype(BF16),
                    w_branch_ssd[l].astype(BF16), w_out[l].astype(BF16), g_m, seq=seq, tm=tm)
        x2 = _moe(x2, sc_f, sh_f, norm_ffn[l].reshape(1, d), g_f, w_router[l], b_router[l],
                  w_gate_up[l], b_gate_up[l], w_down[l], b_down[l], seq=seq, tm=tm)
    return _final_norm(x2, norm_final, tm=tm).reshape(bsz, seq, d)
```

```python
import functools

import jax
import jax.numpy as jnp
from jax import lax
from jax.experimental import pallas as pl
from jax.experimental.pallas import tpu as pltpu

F32 = jnp.float32
BF16 = jnp.bfloat16
I32 = jnp.int32
HIGHEST = lax.Precision.HIGHEST

CHUNK = 64
NORM_EPS = 1e-6
GLA_HEADS, GLA_DK, GLA_DV, GLA_GATE_RANK, GLA_GATE_TAU = 4, 64, 128, 16, 16.0
GDN_HEADS, GDN_DK, GDN_DV, GDN_CONV = 4, 128, 128, 4
SSD_HEADS, SSD_HEAD_DIM, SSD_GROUPS, SSD_STATE, SSD_CONV = 8, 64, 2, 64, 4
SSD_INNER = SSD_HEADS * SSD_HEAD_DIM
N_BRANCH = 3
N_EXPERTS, TOP_K = 32, 4
SWIGLU_LIMIT, SWIGLU_ALPHA = 7.0, 1.702
MOE_BLOCK = 256

LANES = 128
CONV_PAD = 8
GATE_ROWS = 8
MIXER_TILE = 4 * CHUNK
VMEM_LIMIT = 48 * 1024 * 1024


def _params(*sem):
    return pltpu.CompilerParams(dimension_semantics=sem, vmem_limit_bytes=VMEM_LIMIT)


def _mm(a, b):
    return lax.dot_general(a.astype(BF16), b.astype(BF16), (((1,), (0,)), ((), ())),
                           preferred_element_type=F32)


def _mm_nt(a, b):
    return lax.dot_general(a.astype(BF16), b.astype(BF16), (((1,), (1,)), ((), ())),
                           preferred_element_type=F32)


def _mm_tn(a, b):
    return lax.dot_general(a.astype(BF16), b.astype(BF16), (((0,), (0,)), ((), ())),
                           preferred_element_type=F32)


def _mm_f32(a, b):
    return lax.dot_general(a, b, (((1,), (0,)), ((), ())), precision=HIGHEST,
                           preferred_element_type=F32)


def _mm_nt_f32(a, b):
    return lax.dot_general(a, b, (((1,), (1,)), ((), ())), precision=HIGHEST,
                           preferred_element_type=F32)


def _split2(x):
    hi = x.astype(BF16)
    return hi, (x - hi.astype(F32)).astype(BF16)


def _split3(x):
    hi = x.astype(BF16)
    r = x - hi.astype(F32)
    mid = r.astype(BF16)
    return hi, mid, (r - mid.astype(F32)).astype(BF16)


def _mm_split(a, b):
    (a_hi, a_lo), (b_hi, b_lo) = a, b
    m = a_hi.shape[0]
    both = _mm(jnp.concatenate([a_hi, a_lo], axis=0), b_hi)
    return both[:m] + both[m:] + _mm(a_hi, b_lo)


def _sigmoid(x):
    return 1.0 / (1.0 + jnp.exp(-x))


def _silu(x):
    return x * _sigmoid(x)


def _softplus(x):
    return jnp.maximum(x, 0.0) + jnp.log1p(jnp.exp(-jnp.abs(x)))


def _log_sigmoid(x):
    return jnp.minimum(x, 0.0) - jnp.log1p(jnp.exp(-jnp.abs(x)))


def _rms(x):
    return x * lax.rsqrt(jnp.mean(x * x, axis=-1, keepdims=True) + NORM_EPS)


def _norm_mod(x, gain, scale, shift):
    return _rms(x) * gain * (1.0 + scale) + shift


def _chunk_masks():
    row = lax.broadcasted_iota(I32, (CHUNK, CHUNK), 0)
    col = lax.broadcasted_iota(I32, (CHUNK, CHUNK), 1)
    return row >= col, row > col


def _masked_decay(cum_col, cum_row, mask):
    return jnp.where(mask, jnp.exp(jnp.where(mask, cum_col - cum_row, 0.0)), 0.0)


def _mod_kernel(c_ref, w_ref, b_ref, o_ref):
    o_ref[...] = _mm_f32(_silu(c_ref[...]), w_ref[...]) + b_ref[...]


def _modulation(c, w_mod, b_mod):
    depth, d, six_d = w_mod.shape
    bsz = c.shape[0]
    n_col = six_d // d
    return pl.pallas_call(
        _mod_kernel,
        out_shape=jax.ShapeDtypeStruct((depth, bsz, six_d), F32),
        grid=(depth, n_col),
        in_specs=[
            pl.BlockSpec((bsz, d), lambda l, j: (0, 0)),
            pl.BlockSpec((None, d, d), lambda l, j: (l, 0, j)),
            pl.BlockSpec((None, 1, d), lambda l, j: (l, 0, j)),
        ],
        out_specs=pl.BlockSpec((None, bsz, d), lambda l, j: (l, 0, j)),
        compiler_params=_params("arbitrary", "arbitrary"),
        name="adaln_mod",
    )(c, w_mod, b_mod.reshape(depth, 1, six_d))


def _inproj_kernel(x_ref, sc_ref, sh_ref, g_ref, w_ref, *rest, has_t):
    h = _norm_mod(x_ref[...], g_ref[...], sc_ref[...], sh_ref[...]).astype(BF16)
    if has_t:
        wt_ref, o_ref, ot_ref = rest
        ot_ref[...] = _mm_nt(wt_ref[...], h)
    else:
        (o_ref,) = rest
    o_ref[...] = _mm(h, w_ref[...])


def _inproj(x2, scale, shift, gain, w, wt, *, seq, tm):
    n, d = x2.shape
    width = w.shape[1]
    per_batch = seq // tm
    batch_spec = pl.BlockSpec((None, 1, d), lambda i: (i // per_batch, 0, 0))
    in_specs = [
        pl.BlockSpec((tm, d), lambda i: (i, 0)),
        batch_spec, batch_spec,
        pl.BlockSpec((1, d), lambda i: (0, 0)),
        pl.BlockSpec((d, width), lambda i: (0, 0)),
    ]
    args = [x2, scale, shift, gain, w]
    out_shape = [jax.ShapeDtypeStruct((n, width), F32)]
    out_specs = [pl.BlockSpec((tm, width), lambda i: (i, 0))]
    if wt is not None:
        in_specs.append(pl.BlockSpec((GATE_ROWS, d), lambda i: (0, 0)))
        args.append(wt)
        out_shape.append(jax.ShapeDtypeStruct((GATE_ROWS, n), F32))
        out_specs.append(pl.BlockSpec((GATE_ROWS, tm), lambda i: (0, i)))
    outs = pl.pallas_call(
        functools.partial(_inproj_kernel, has_t=wt is not None),
        out_shape=out_shape, grid=(n // tm,), in_specs=in_specs, out_specs=out_specs,
        compiler_params=_params("arbitrary"),
        name="inproj",
    )(*args)
    return outs if wt is not None else outs[0]


GLA_QK = GLA_HEADS * GLA_DK
GLA_V = GLA_HEADS * GLA_DV
GLA_WIDTH = 2 * GLA_QK + 2 * GLA_V + LANES


def _chunk_cumsum_matrices(tile):
    r = jnp.arange(tile)
    tri = ((r[:, None] // CHUNK == r[None, :] // CHUNK) & (r[:, None] >= r[None, :])).astype(BF16)
    tri3 = jnp.concatenate([tri, tri, tri], axis=1)
    return tri3, tri3.T


def _gla_kernel(p_ref, tri_ref, w2_ref, b2_ref, ng_ref, o_ref, st_ref, oin_ref, qd_ref):
    tile = p_ref.shape[0]
    heads = range(GLA_HEADS)

    @pl.when(pl.program_id(1) == 0)
    def _():
        st_ref[...] = jnp.zeros_like(st_ref)

    incl, _ = _chunk_masks()
    lr = p_ref[:, 2 * GLA_QK + 2 * GLA_V:GLA_WIDTH]
    gk = _log_sigmoid(_mm(lr, w2_ref[...]) + b2_ref[...]) / GLA_GATE_TAU
    bcum = _mm(tri_ref[...], jnp.concatenate(_split3(gk), axis=0))
    ks = [slice(h * GLA_DK, (h + 1) * GLA_DK) for h in heads]
    vs = [slice(h * GLA_DV, (h + 1) * GLA_DV) for h in heads]

    upds, decays = [], []
    for c in range(tile // CHUNK):
        rows = slice(c * CHUNK, (c + 1) * CHUNK)
        b = bcum[rows]
        b_last = b[CHUNK - 1:CHUNK, :]
        q_dec = (p_ref[rows, 0:GLA_QK] * (GLA_DK ** -0.5) * jnp.exp(b)).astype(BF16)
        k = p_ref[rows, GLA_QK:2 * GLA_QK]
        k_neg = (k * jnp.exp(-b)).astype(BF16)
        k_pos = (k * jnp.exp(b_last - b)).astype(BF16)
        v = [p_ref[rows, 2 * GLA_QK + h * GLA_DV:2 * GLA_QK + (h + 1) * GLA_DV].astype(BF16)
             for h in heads]
        atts = [jnp.where(incl, _mm_nt(q_dec[:, ks[h]], k_neg[:, ks[h]]), 0.0) for h in heads]
        for h in heads:
            oin_ref[rows, vs[h]] = _mm(atts[h], v[h])
        upds.append([_mm_tn(v[h], k_pos[:, ks[h]]) for h in heads])
        decays.append(jnp.exp(b_last))
        qd_ref[rows, :] = q_dec

    for c in range(tile // CHUNK):
        rows = slice(c * CHUNK, (c + 1) * CHUNK)
        sts = [st_ref[h] for h in heads]
        os = [oin_ref[rows, vs[h]] + _mm_nt(qd_ref[rows, ks[h]], sts[h]) for h in heads]
        for h in heads:
            st_ref[h] = sts[h] * decays[c][:, ks[h]] + upds[c][h]
        for h in heads:
            rs = slice(2 * GLA_QK + GLA_V + h * GLA_DV, 2 * GLA_QK + GLA_V + (h + 1) * GLA_DV)
            o = _rms(os[h]) * ng_ref[...]
            o_ref[rows, vs[h]] = (o * _silu(p_ref[rows, rs])).astype(o_ref.dtype)


def _gla(p, w2, b2, ng, *, bsz, seq, tile):
    nt = seq // tile
    tri3, _ = _chunk_cumsum_matrices(tile)
    const = lambda shape: pl.BlockSpec(shape, lambda b, c: (0,) * len(shape))
    return pl.pallas_call(
        _gla_kernel,
        out_shape=jax.ShapeDtypeStruct((bsz * seq, GLA_V), BF16),
        grid=(bsz, nt),
        in_specs=[
            pl.BlockSpec((tile, GLA_WIDTH), lambda b, c: (b * nt + c, 0)),
            const(tri3.shape), const((LANES, GLA_QK)), const((1, GLA_QK)), const((1, GLA_DV)),
        ],
        out_specs=pl.BlockSpec((tile, GLA_V), lambda b, c: (b * nt + c, 0)),
        scratch_shapes=[
            pltpu.VMEM((GLA_HEADS, GLA_DV, GLA_DK), F32),
            pltpu.VMEM((tile, GLA_V), F32),
            pltpu.VMEM((tile, GLA_QK), BF16),
        ],
        compiler_params=_params("arbitrary", "arbitrary"),
        name="gla_mixer",
    )(p, tri3, w2, b2, ng)


def _conv_window(ext_ref, w_ref, r0, taps):
    out = None
    for j in range(taps):
        start = CONV_PAD + r0 - (taps - 1) + j
        term = ext_ref[start:start + CHUNK, :] * w_ref[j:j + 1, :]
        out = term if out is None else out + term
    return out


SSD_BC = SSD_GROUPS * SSD_STATE
SSD_XBC = SSD_INNER + 2 * SSD_BC
SSD_WIDTH = SSD_INNER + SSD_XBC + LANES
SSD_HG = SSD_HEADS // SSD_GROUPS


def _ssd_kernel(p_ref, dtt_ref, tri_ref, triu_ref, expand_ref, cw_ref, cb_ref, alog_e_ref, dtb_ref,
                alog_c_ref, dtb_c_ref, dskip_ref, ng_ref, o_ref, ext_ref, st_ref, y_ref, xs_ref, cm_ref):
    tile = p_ref.shape[0]
    heads = range(SSD_HEADS)
    groups = range(SSD_GROUPS)
    gw = SSD_INNER // SSD_GROUPS

    @pl.when(pl.program_id(1) == 0)
    def _():
        st_ref[...] = jnp.zeros_like(st_ref)
        ext_ref[0:CONV_PAD, :] = jnp.zeros((CONV_PAD, SSD_XBC), F32)

    incl, _ = _chunk_masks()
    ext_ref[CONV_PAD:CONV_PAD + tile, :] = p_ref[:, SSD_INNER:SSD_INNER + SSD_XBC]
    dt_col = _softplus(p_ref[:, SSD_INNER + SSD_XBC:SSD_WIDTH] + dtb_ref[...])
    dt_e = _mm(jnp.concatenate(_split3(dt_col), axis=1), expand_ref[...])
    cum_e = _mm(tri_ref[...], jnp.concatenate(_split3(dt_e * (-jnp.exp(alog_e_ref[...]))), axis=0))
    dt_row = _softplus(dtt_ref[...] + dtb_c_ref[...])
    cum_row = _mm(jnp.concatenate(_split3(dt_row * (-jnp.exp(alog_c_ref[...]))), axis=1),
                  triu_ref[...])
    hs = [slice(h * SSD_HEAD_DIM, (h + 1) * SSD_HEAD_DIM) for h in heads]
    gs = [slice(g * gw, (g + 1) * gw) for g in groups]

    upds, st_decays = [], []
    for c in range(tile // CHUNK):
        r0 = c * CHUNK
        rows = slice(r0, r0 + CHUNK)
        xbc = _silu(_conv_window(ext_ref, cw_ref, r0, SSD_CONV) + cb_ref[...])
        xs = xbc[:, 0:SSD_INNER]
        xs_ref[rows, :] = xs
        cm_ref[rows, :] = xbc[:, SSD_INNER + SSD_BC:SSD_XBC].astype(BF16)
        bm = [xbc[:, SSD_INNER + g * SSD_STATE:SSD_INNER + (g + 1) * SSD_STATE] for g in groups]
        cm = [xbc[:, SSD_INNER + SSD_BC + g * SSD_STATE:SSD_INNER + SSD_BC + (g + 1) * SSD_STATE]
              for g in groups]
        ce = cum_e[rows]
        cle = ce[CHUNK - 1:CHUNK, :]
        xdt = xs * dt_e[rows]
        xdt_bf = xdt.astype(BF16)
        xdec = (xdt * jnp.exp(cle - ce)).astype(BF16)
        cb = [_mm_nt(cm[g], bm[g]) for g in groups]
        segs = [_masked_decay(ce[:, hs[h]], cum_row[h:h + 1, rows], incl) for h in heads]
        for h in heads:
            y_ref[rows, hs[h]] = _mm(cb[h // SSD_HG] * segs[h], xdt_bf[:, hs[h]])
        upds.append([_mm_tn(bm[g], xdec[:, gs[g]]) for g in groups])
        st_decays.append(jnp.exp(cle))
    ext_ref[0:CONV_PAD, :] = ext_ref[tile:tile + CONV_PAD, :]

    for c in range(tile // CHUNK):
        rows = slice(c * CHUNK, (c + 1) * CHUNK)
        sts = [st_ref[g] for g in groups]
        y_in = [_mm(cm_ref[rows, g * SSD_STATE:(g + 1) * SSD_STATE], sts[g]) for g in groups]
        for g in groups:
            st_ref[g] = st_decays[c][:, gs[g]] * sts[g] + upds[c][g]
        y = y_ref[rows, :] + jnp.concatenate(y_in, axis=1) * jnp.exp(cum_e[rows])
        y = (y + dskip_ref[...] * xs_ref[rows, :]) * _silu(p_ref[rows, 0:SSD_INNER])
        for g in groups:
            o_ref[rows, gs[g]] = (_rms(y[:, gs[g]]) * ng_ref[:, gs[g]]).astype(o_ref.dtype)


def _ssd(p, dtt, conv_w, conv_b, a_log, dt_bias, d_skip, ng, *, bsz, seq, tile):
    nt = seq // tile
    pad = lambda v: jnp.pad(v, (0, LANES - v.shape[0])).reshape(1, LANES)
    col = lambda v: v.reshape(SSD_HEADS, 1)
    const = lambda shape: pl.BlockSpec(shape, lambda b, c: (0,) * len(shape))
    tri3, triu3 = _chunk_cumsum_matrices(tile)
    per_channel = lambda v: jnp.repeat(v, SSD_HEAD_DIM).reshape(1, SSD_INNER)
    expand = (jnp.arange(LANES)[:, None] == jnp.arange(SSD_INNER)[None, :] // SSD_HEAD_DIM)
    expand3 = jnp.concatenate([expand.astype(BF16)] * 3, axis=0)
    return pl.pallas_call(
        _ssd_kernel,
        out_shape=jax.ShapeDtypeStruct((bsz * seq, SSD_INNER), BF16),
        grid=(bsz, nt),
        in_specs=[
            pl.BlockSpec((tile, SSD_WIDTH), lambda b, c: (b * nt + c, 0)),
            pl.BlockSpec((GATE_ROWS, tile), lambda b, c: (0, b * nt + c)),
            const(tri3.shape), const(triu3.shape), const(expand3.shape),
            const((SSD_CONV, SSD_XBC)), const((1, SSD_XBC)),
            const((1, SSD_INNER)), const((1, LANES)),
            const((SSD_HEADS, 1)), const((SSD_HEADS, 1)),
            const((1, SSD_INNER)), const((1, SSD_INNER)),
        ],
        out_specs=pl.BlockSpec((tile, SSD_INNER), lambda b, c: (b * nt + c, 0)),
        scratch_shapes=[
            pltpu.VMEM((CONV_PAD + tile, SSD_XBC), F32),
            pltpu.VMEM((SSD_GROUPS, SSD_STATE, SSD_INNER // SSD_GROUPS), F32),
            pltpu.VMEM((tile, SSD_INNER), F32),
            pltpu.VMEM((tile, SSD_INNER), F32),
            pltpu.VMEM((tile, SSD_BC), BF16),
        ],
        compiler_params=_params("arbitrary", "arbitrary"),
        name="ssd_mixer",
    )(p, dtt, tri3, triu3, expand3, conv_w, conv_b.reshape(1, SSD_XBC), per_channel(a_log),
      pad(dt_bias), col(a_log), col(dt_bias), per_channel(d_skip), ng.reshape(1, SSD_INNER))


GDN_QK = GDN_HEADS * GDN_DK
GDN_V = GDN_HEADS * GDN_DV
GDN_QKV = 2 * GDN_QK + GDN_V
GDN_WIDTH = GDN_QKV + GDN_V + LANES


def _l2norm(x):
    return x * lax.rsqrt(jnp.sum(x * x, axis=-1, keepdims=True) + 1e-6)


def _unit_lower_inverses(mats):
    shape = mats[0].shape
    eye = jnp.where(lax.broadcasted_iota(I32, shape, 0) == lax.broadcasted_iota(I32, shape, 1),
                    1.0, 0.0)
    ps = [eye - a for a in mats]
    n_splits = [_split2(-a) for a in mats]
    steps = (shape[0] - 1).bit_length() - 1
    for _ in range(steps):
        n_splits = [_split2(_mm_split(ns, ns)) for ns in n_splits]
        ps = [p + _mm_split(_split2(p), ns) for p, ns in zip(ps, n_splits)]
    return ps


def _gdn_kernel(p_ref, abt_ref, tri_ref, triu_ref, cw_ref, alog_ref, dtb_ref, alog_c_ref, dtb_c_ref,
                ng_ref, o_ref, ext_ref, st_ref, upre_ref, wmix_ref, qdec_ref, kdec_ref, pmat_ref):
    tile = p_ref.shape[0]

    @pl.when(pl.program_id(1) == 0)
    def _():
        st_ref[...] = jnp.zeros_like(st_ref)
        ext_ref[0:CONV_PAD, :] = jnp.zeros((CONV_PAD, GDN_QKV), F32)

    incl, strict = _chunk_masks()
    ext_ref[CONV_PAD:CONV_PAD + tile, :] = p_ref[:, 0:GDN_QKV]
    ab = p_ref[:, GDN_QKV + GDN_V:GDN_WIDTH]
    beta_all = _sigmoid(ab)
    g_col = -jnp.exp(alog_ref[...]) * _softplus(ab + dtb_ref[...])
    cum_col = _mm(tri_ref[...], jnp.concatenate(_split3(g_col), axis=0))
    g_row = -jnp.exp(alog_c_ref[...]) * _softplus(abt_ref[...] + dtb_c_ref[...])
    cum_row = _mm(jnp.concatenate(_split3(g_row), axis=1), triu_ref[...])

    a_mats, rhss, where = [], [], []
    for c in range(tile // CHUNK):
        r0 = c * CHUNK
        rows = slice(r0, r0 + CHUNK)
        qkv = _silu(_conv_window(ext_ref, cw_ref, r0, GDN_CONV))
        for h in range(GDN_HEADS):
            hs = slice(h * GDN_DV, (h + 1) * GDN_DV)
            q = _l2norm(qkv[:, h * GDN_DK:(h + 1) * GDN_DK]) * (GDN_DK ** -0.5)
            k = _l2norm(qkv[:, GDN_QK + h * GDN_DK:GDN_QK + (h + 1) * GDN_DK])
            v = qkv[:, 2 * GDN_QK + h * GDN_DV:2 * GDN_QK + (h + 1) * GDN_DV]
            beta = beta_all[rows, GDN_HEADS + h:GDN_HEADS + h + 1]
            cc = cum_col[rows, h:h + 1]
            cl = cum_col[r0 + CHUNK - 1:r0 + CHUNK, h:h + 1]
            gam = _masked_decay(cc, cum_row[h:h + 1, rows], incl)
            a_mats.append(jnp.where(strict, beta * _mm_nt(k, k) * gam, 0.0))
            rhss.append(_split2(jnp.concatenate([beta * v, (beta * jnp.exp(cc)) * k], axis=1)))
            where.append((rows, hs))
            qdec_ref[rows, hs] = (q * jnp.exp(cc)).astype(BF16)
            kdec_ref[rows, hs] = (k * jnp.exp(cl - cc)).astype(BF16)
            pmat_ref[rows, h * CHUNK:(h + 1) * CHUNK] = (_mm_nt(q, k) * gam).astype(BF16)
    ext_ref[0:CONV_PAD, :] = ext_ref[tile:tile + CONV_PAD, :]
    t_invs = _unit_lower_inverses(a_mats)
    for t_inv, rhs, (rows, hs) in zip(t_invs, rhss, where):
        sol = _mm_split(_split2(t_inv), rhs)
        upre_ref[rows, hs] = sol[:, 0:GDN_DV]
        wmix_ref[rows, hs] = sol[:, GDN_DV:2 * GDN_DV].astype(BF16)

    heads = range(GDN_HEADS)
    for c in range(tile // CHUNK):
        r0 = c * CHUNK
        rows = slice(r0, r0 + CHUNK)
        hss = [slice(h * GDN_DV, (h + 1) * GDN_DV) for h in heads]
        ms = [st_ref[h] for h in heads]
        us = [upre_ref[rows, hss[h]] - _mm(wmix_ref[rows, hss[h]], ms[h]) for h in heads]
        os = [_mm(qdec_ref[rows, hss[h]], ms[h])
              + _mm(pmat_ref[rows, h * CHUNK:(h + 1) * CHUNK], us[h]) for h in heads]
        for h in heads:
            cl = cum_col[r0 + CHUNK - 1:r0 + CHUNK, h:h + 1]
            st_ref[h] = jnp.exp(cl) * ms[h] + _mm_tn(kdec_ref[rows, hss[h]], us[h])
        for h in heads:
            o = _rms(os[h]) * ng_ref[...]
            gs = slice(GDN_QKV + h * GDN_DV, GDN_QKV + (h + 1) * GDN_DV)
            o_ref[rows, hss[h]] = (o * _silu(p_ref[rows, gs])).astype(o_ref.dtype)


def _gdn(p, abt, conv_w, a_log, dt_bias, ng, *, bsz, seq, tile):
    nt = seq // tile
    pad = lambda v: jnp.pad(v, (0, LANES - v.shape[0])).reshape(1, LANES)
    col = lambda v: jnp.pad(v, (0, GATE_ROWS - v.shape[0])).reshape(GATE_ROWS, 1)
    const = lambda shape: pl.BlockSpec(shape, lambda b, c: (0,) * len(shape))
    tri3, triu3 = _chunk_cumsum_matrices(tile)
    return pl.pallas_call(
        _gdn_kernel,
        out_shape=jax.ShapeDtypeStruct((bsz * seq, GDN_V), BF16),
        grid=(bsz, nt),
        in_specs=[
            pl.BlockSpec((tile, GDN_WIDTH), lambda b, c: (b * nt + c, 0)),
            pl.BlockSpec((GATE_ROWS, tile), lambda b, c: (0, b * nt + c)),
            const(tri3.shape), const(triu3.shape),
            const((GDN_CONV, GDN_QKV)),
            const((1, LANES)), const((1, LANES)),
            const((GATE_ROWS, 1)), const((GATE_ROWS, 1)),
            const((1, GDN_DV)),
        ],
        out_specs=pl.BlockSpec((tile, GDN_V), lambda b, c: (b * nt + c, 0)),
        scratch_shapes=[
            pltpu.VMEM((CONV_PAD + tile, GDN_QKV), F32),
            pltpu.VMEM((GDN_HEADS, GDN_DK, GDN_DV), F32),
            pltpu.VMEM((tile, GDN_V), F32),
            pltpu.VMEM((tile, GDN_V), BF16),
            pltpu.VMEM((tile, GDN_QK), BF16),
            pltpu.VMEM((tile, GDN_QK), BF16),
            pltpu.VMEM((tile, GDN_HEADS * CHUNK), BF16),
        ],
        compiler_params=_params("arbitrary", "arbitrary"),
        name="gdn_mixer",
    )(p, abt, tri3, triu3, conv_w, pad(a_log), pad(dt_bias), col(a_log), col(dt_bias),
      ng.reshape(1, GDN_DV))


def _merge_kernel(x_ref, ya_ref, yb_ref, yc_ref, mr_ref, bm_ref, wa_ref, wb_ref, wc_ref, wo_ref,
                  gm_ref, o_ref):
    d = x_ref.shape[1]
    merged = None
    for i, (y_ref, w_ref) in enumerate(((ya_ref, wa_ref), (yb_ref, wb_ref), (yc_ref, wc_ref))):
        gate = _sigmoid(mr_ref[:, i * d:(i + 1) * d] + bm_ref[:, i * d:(i + 1) * d])
        term = gate * _mm(y_ref[...], w_ref[...])
        merged = term if merged is None else merged + term
    o_ref[...] = x_ref[...] + gm_ref[...] * _mm(merged, wo_ref[...])


def _merge(x2, ya, yb, yc, mr, b_merge, wa, wb, wc, wo, gm, *, seq, tm):
    n, d = x2.shape
    per_batch = seq // tm
    row = lambda w: pl.BlockSpec((tm, w), lambda i: (i, 0))
    const = lambda shape: pl.BlockSpec(shape, lambda i: (0,) * len(shape))
    return pl.pallas_call(
        _merge_kernel,
        out_shape=jax.ShapeDtypeStruct((n, d), F32),
        grid=(n // tm,),
        in_specs=[
            row(d), row(ya.shape[1]), row(yb.shape[1]), row(yc.shape[1]), row(N_BRANCH * d),
            const((1, N_BRANCH * d)),
            const(wa.shape), const(wb.shape), const(wc.shape), const(wo.shape),
            pl.BlockSpec((None, 1, d), lambda i: (i // per_batch, 0, 0)),
        ],
        out_specs=row(d),
        compiler_params=_params("arbitrary"),
        name="merge_out",
    )(x2, ya, yb, yc, mr, b_merge.reshape(1, N_BRANCH * d), wa, wb, wc, wo, gm)


def _router_kernel(x_ref, sc_ref, sh_ref, g_ref, wr_ref, br_ref,
                   h_ref, idx_ref, gate_ref, rank_ref, cnt_ref, carry_ref):
    @pl.when(pl.program_id(0) == 0)
    def _():
        carry_ref[...] = jnp.zeros_like(carry_ref)

    tm = x_ref.shape[0]
    h = _norm_mod(x_ref[...], g_ref[...], sc_ref[...], sh_ref[...])
    h_ref[...] = h
    logits = _mm_nt_f32(wr_ref[...], h) + br_ref[...]
    e_iota = lax.broadcasted_iota(I32, logits.shape, 0).astype(F32)
    vals, idxs, sels = [], [], []
    cur = logits
    for _ in range(TOP_K):
        m = jnp.max(cur, axis=0, keepdims=True)
        idx = jnp.min(jnp.where(cur == m, e_iota, float(N_EXPERTS)), axis=0, keepdims=True)
        sel = e_iota == idx
        cur = jnp.where(sel, -jnp.inf, cur)
        vals.append(m)
        idxs.append(idx.astype(I32))
        sels.append(sel)
    exps = [jnp.exp(v - vals[0]) for v in vals]
    denom = exps[0] + exps[1] + exps[2] + exps[3]
    onehot = jnp.zeros(logits.shape, F32)
    for sel in sels:
        onehot = onehot + jnp.where(sel, 1.0, 0.0)
    row = lax.broadcasted_iota(I32, (tm, tm), 0)
    col = lax.broadcasted_iota(I32, (tm, tm), 1)
    before = _mm(onehot, jnp.where(row < col, 1.0, 0.0)) + carry_ref[...]
    zeros_i = jnp.zeros((GATE_ROWS - TOP_K, tm), I32)
    idx_ref[...] = jnp.concatenate(idxs + [zeros_i], axis=0)
    gate_ref[...] = jnp.concatenate([e / denom for e in exps] + [zeros_i.astype(F32)], axis=0)
    ranks = [jnp.sum(jnp.where(sel, before, 0.0), axis=0, keepdims=True).astype(I32) for sel in sels]
    rank_ref[...] = jnp.concatenate(ranks + [zeros_i], axis=0)
    carry_ref[...] = carry_ref[...] + jnp.sum(onehot, axis=1, keepdims=True)
    cnt_ref[...] = jnp.broadcast_to(carry_ref[...], cnt_ref.shape).astype(I32)


def _router(x2, scale, shift, gain, w_router_t, b_router, *, seq, tm):
    n, d = x2.shape
    per_batch = seq // tm
    batch_spec = pl.BlockSpec((None, 1, d), lambda i: (i // per_batch, 0, 0))
    lane_out = pl.BlockSpec((GATE_ROWS, tm), lambda i: (0, i))
    return pl.pallas_call(
        _router_kernel,
        out_shape=[
            jax.ShapeDtypeStruct((n, d), F32),
            jax.ShapeDtypeStruct((GATE_ROWS, n), I32),
            jax.ShapeDtypeStruct((GATE_ROWS, n), F32),
            jax.ShapeDtypeStruct((GATE_ROWS, n), I32),
            jax.ShapeDtypeStruct((N_EXPERTS, LANES), I32),
        ],
        grid=(n // tm,),
        in_specs=[
            pl.BlockSpec((tm, d), lambda i: (i, 0)),
            batch_spec, batch_spec,
            pl.BlockSpec((1, d), lambda i: (0, 0)),
            pl.BlockSpec((N_EXPERTS, d), lambda i: (0, 0)),
            pl.BlockSpec((N_EXPERTS, 1), lambda i: (0, 0)),
        ],
        out_specs=[
            pl.BlockSpec((tm, d), lambda i: (i, 0)),
            lane_out, lane_out, lane_out,
            pl.BlockSpec((N_EXPERTS, LANES), lambda i: (0, 0)),
        ],
        scratch_shapes=[pltpu.VMEM((N_EXPERTS, 1), F32)],
        compiler_params=_params("arbitrary"),
        name="router",
    )(x2, scale, shift, gain, w_router_t, b_router.reshape(N_EXPERTS, 1))


def _row_copy(src, src_row, dst, dst_row, sem):
    return pltpu.make_async_copy(src.at[pl.ds(src_row, 1), :], dst.at[pl.ds(dst_row, 1), :], sem)


def _load_positions(pos_hbm, pos_smem, sem, count):
    start = pl.multiple_of(pl.program_id(0) * count, count)
    cp = pltpu.make_async_copy(pos_hbm.at[pl.ds(start, count)], pos_smem, sem)
    cp.start()
    cp.wait()


def _dispatch_kernel(pos_hbm, h_ref, xs_init, xs_hbm, pos_smem, pos_sem, row_sem):
    del xs_init
    td = h_ref.shape[0]
    _load_positions(pos_hbm, pos_smem, pos_sem, TOP_K * td)

    def start(t, carry):
        for k in range(TOP_K):
            _row_copy(h_ref, t, xs_hbm, pos_smem[TOP_K * t + k], row_sem).start()
        return carry

    def wait(t, carry):
        for k in range(TOP_K):
            _row_copy(h_ref, 0, xs_hbm, 0, row_sem).wait()
        return carry

    lax.fori_loop(0, td, start, 0)
    lax.fori_loop(0, td, wait, 0)


def _dispatch(pos_flat, h2, n_rows, *, td):
    n, d = h2.shape
    xs0 = jnp.zeros((n_rows, d), F32)
    return pl.pallas_call(
        _dispatch_kernel,
        out_shape=jax.ShapeDtypeStruct((n_rows, d), F32),
        grid=(n // td,),
        in_specs=[
            pl.BlockSpec(memory_space=pl.ANY),
            pl.BlockSpec((td, d), lambda i: (i, 0)),
            pl.BlockSpec(memory_space=pl.ANY),
        ],
        out_specs=pl.BlockSpec(memory_space=pl.ANY),
        scratch_shapes=[
            pltpu.SMEM((TOP_K * td,), I32),
            pltpu.SemaphoreType.DMA,
            pltpu.SemaphoreType.DMA,
        ],
        input_output_aliases={2: 0},
        compiler_params=_params("arbitrary"),
        name="moe_dispatch",
    )(pos_flat, h2, xs0)


PAIR_TILE = 2 * LANES


def _pair_regroup_kernel(w_ref, o_ref):
    r = lax.broadcasted_iota(I32, (PAIR_TILE, PAIR_TILE), 0)
    c = lax.broadcasted_iota(I32, (PAIR_TILE, PAIR_TILE), 1)
    src = jnp.where(c < LANES, 2 * c, 2 * (c - LANES) + 1)
    perm = jnp.where(r == src, 1.0, 0.0).astype(BF16)
    for t in range(w_ref.shape[1] // PAIR_TILE):
        cols = slice(t * PAIR_TILE, (t + 1) * PAIR_TILE)
        o_ref[:, cols] = _mm(w_ref[:, cols], perm).astype(o_ref.dtype)


def _pair_regroup(w, *, tr):
    e, d, width = w.shape
    return pl.pallas_call(
        _pair_regroup_kernel,
        out_shape=jax.ShapeDtypeStruct(w.shape, BF16),
        grid=(e, d // tr),
        in_specs=[pl.BlockSpec((None, tr, width), lambda i, j: (i, j, 0))],
        out_specs=pl.BlockSpec((None, tr, width), lambda i, j: (i, j, 0)),
        compiler_params=_params("arbitrary", "arbitrary"),
        name="gate_up_regroup",
    )(w)


def _expert_kernel(be_ref, x_ref, wgu_ref, bgu_ref, wd_ref, bd_ref, o_ref):
    del be_ref
    f = wd_ref.shape[0]
    gu = _mm(x_ref[...], wgu_ref[...]) + bgu_ref[...]
    acts = []
    for t in range(f // LANES):
        gate = jnp.minimum(gu[:, t * PAIR_TILE:t * PAIR_TILE + LANES], SWIGLU_LIMIT)
        up = jnp.clip(gu[:, t * PAIR_TILE + LANES:(t + 1) * PAIR_TILE], -SWIGLU_LIMIT, SWIGLU_LIMIT)
        acts.append(((up + 1.0) * gate * _sigmoid(SWIGLU_ALPHA * gate)).astype(BF16))
    o_ref[...] = _mm(jnp.concatenate(acts, axis=1), wd_ref[...]) + bd_ref[...]


def _experts(block_e, xs, wgu, bgu, wd, bd):
    n_rows, d = xs.shape
    f = wd.shape[1]
    n_blocks = n_rows // MOE_BLOCK
    grid_spec = pltpu.PrefetchScalarGridSpec(
        num_scalar_prefetch=1,
        grid=(n_blocks,),
        in_specs=[
            pl.BlockSpec((MOE_BLOCK, d), lambda i, be: (i, 0)),
            pl.BlockSpec((None, d, 2 * f), lambda i, be: (be[i], 0, 0)),
            pl.BlockSpec((None, 1, 2 * f), lambda i, be: (be[i], 0, 0)),
            pl.BlockSpec((None, f, d), lambda i, be: (be[i], 0, 0)),
            pl.BlockSpec((None, 1, d), lambda i, be: (be[i], 0, 0)),
        ],
        out_specs=pl.BlockSpec((MOE_BLOCK, d), lambda i, be: (i, 0)),
    )
    return pl.pallas_call(
        _expert_kernel,
        out_shape=jax.ShapeDtypeStruct((n_rows, d), F32),
        grid_spec=grid_spec,
        compiler_params=_params("arbitrary"),
        name="moe_experts",
    )(block_e, xs, wgu, bgu, wd, bd)


def _combine_kernel(pos_hbm, ys_hbm, gate_ref, x_ref, gf_ref, o_ref, buf_ref, pos_smem, pos_sem,
                    row_sem):
    tc = x_ref.shape[0]
    _load_positions(pos_hbm, pos_smem, pos_sem, TOP_K * tc)

    def start(t, carry):
        for k in range(TOP_K):
            _row_copy(ys_hbm, pos_smem[TOP_K * t + k], buf_ref.at[k], t, row_sem).start()
        return carry

    def wait(t, carry):
        for k in range(TOP_K):
            _row_copy(ys_hbm, 0, buf_ref.at[k], 0, row_sem).wait()
        return carry

    lax.fori_loop(0, tc, start, 0)
    lax.fori_loop(0, tc, wait, 0)
    y = None
    for k in range(TOP_K):
        term = gate_ref[:, k:k + 1] * buf_ref[k]
        y = term if y is None else y + term
    o_ref[...] = x_ref[...] + gf_ref[...] * y


def _combine(pos_flat, ys, gates, x2, gf, *, seq, tc):
    n, d = x2.shape
    per_batch = seq // tc
    return pl.pallas_call(
        _combine_kernel,
        out_shape=jax.ShapeDtypeStruct((n, d), F32),
        grid=(n // tc,),
        in_specs=[
            pl.BlockSpec(memory_space=pl.ANY),
            pl.BlockSpec(memory_space=pl.ANY),
            pl.BlockSpec((tc, TOP_K), lambda i: (i, 0)),
            pl.BlockSpec((tc, d), lambda i: (i, 0)),
            pl.BlockSpec((None, 1, d), lambda i: (i // per_batch, 0, 0)),
        ],
        out_specs=pl.BlockSpec((tc, d), lambda i: (i, 0)),
        scratch_shapes=[
            pltpu.VMEM((TOP_K, tc, d), F32),
            pltpu.SMEM((TOP_K * tc,), I32),
            pltpu.SemaphoreType.DMA,
            pltpu.SemaphoreType.DMA,
        ],
        compiler_params=_params("arbitrary"),
        name="moe_combine",
    )(pos_flat, ys, gates, x2, gf)


def _final_norm_kernel(x_ref, g_ref, o_ref):
    o_ref[...] = _rms(x_ref[...]) * g_ref[...]


def _final_norm(x2, gain, *, tm):
    n, d = x2.shape
    return pl.pallas_call(
        _final_norm_kernel,
        out_shape=jax.ShapeDtypeStruct((n, d), F32),
        grid=(n // tm,),
        in_specs=[pl.BlockSpec((tm, d), lambda i: (i, 0)), pl.BlockSpec((1, d), lambda i: (0, 0))],
        out_specs=pl.BlockSpec((tm, d), lambda i: (i, 0)),
        compiler_params=_params("arbitrary"),
        name="final_norm",
    )(x2, gain.reshape(1, d))


def _split_w_in(w_in):
    d = w_in.shape[0]
    sizes = (GLA_QK, GLA_QK, GLA_V, GLA_GATE_RANK, GLA_V, GDN_QKV, GDN_HEADS, GDN_HEADS, GDN_V,
             SSD_INNER, SSD_XBC, SSD_HEADS, N_BRANCH * d)
    parts, acc = [], 0
    for s in sizes:
        parts.append(w_in[:, acc:acc + s])
        acc += s
    (gq, gk, gv, glr, gr, dqkv, da, db, dg, sz, sxbc, sdt, mrg) = parts
    zeros = lambda w: jnp.zeros((d, w), w_in.dtype)
    w_gla = jnp.concatenate([gq, gk, gv, gr, glr, zeros(LANES - GLA_GATE_RANK)], axis=1)
    w_gdn = jnp.concatenate([dqkv, dg, da, db, zeros(LANES - 2 * GDN_HEADS)], axis=1)
    w_ssd = jnp.concatenate([sz, sxbc, sdt, zeros(LANES - SSD_HEADS)], axis=1)
    pad_rows = lambda m: jnp.pad(m, ((0, GATE_ROWS - m.shape[0]), (0, 0)))
    w_gdn_t = pad_rows(jnp.concatenate([da, db], axis=1).T)
    w_ssd_t = pad_rows(sdt.T)
    cast = lambda m: m.astype(BF16)
    return cast(w_gla), cast(w_gdn), cast(w_gdn_t), cast(w_ssd), cast(w_ssd_t), cast(mrg)


def _moe(x2, scale, shift, gain, gf, w_router, b_router, w_gate_up, b_gate_up, w_down, b_down, *,
         seq, tm):
    n, d = x2.shape
    h2, idx_t, gate_t, rank_t, counts = _router(x2, scale, shift, gain, w_router.T, b_router,
                                                seq=seq, tm=tm)
    counts = counts[:, 0]
    padded = ((counts + MOE_BLOCK - 1) // MOE_BLOCK) * MOE_BLOCK
    pad_end = jnp.cumsum(padded)
    pad_start = pad_end - padded
    n_blocks = -(-(n * TOP_K) // MOE_BLOCK) + N_EXPERTS
    block_start = jnp.arange(n_blocks, dtype=I32) * MOE_BLOCK
    block_e = jnp.minimum(jnp.sum((pad_end[None, :] <= block_start[:, None]).astype(I32), axis=1),
                          N_EXPERTS - 1)
    experts = jnp.arange(N_EXPERTS, dtype=I32)
    start_of = jnp.sum(jnp.where(idx_t[:TOP_K, :, None] == experts, pad_start, 0), axis=-1)
    pos = (start_of + rank_t[:TOP_K]).astype(I32)
    pos_flat = pos.T.reshape(-1)
    gates = gate_t[:TOP_K].T
    f = w_down.shape[1]
    wgu = _pair_regroup(w_gate_up, tr=min(512, d))
    bgu = b_gate_up.reshape(N_EXPERTS, 2 * f // PAIR_TILE, LANES, 2).transpose(0, 1, 3, 2)
    xs = _dispatch(pos_flat, h2, n_blocks * MOE_BLOCK, td=MOE_BLOCK)
    ys = _experts(block_e, xs, wgu, bgu.reshape(N_EXPERTS, 1, 2 * f), w_down.astype(BF16),
                  b_down.reshape(N_EXPERTS, 1, d))
    return _combine(pos_flat, ys, gates, x2, gf, seq=seq, tc=MOE_BLOCK)


def kernel(x, c, w_mod, b_mod, norm_mix, norm_ffn, norm_final, w_in, gla_w_gate2, gla_b_gate2, gla_norm, gdn_conv_w, gdn_a_log, gdn_dt_bias, gdn_norm, ssd_conv_w, ssd_conv_b, ssd_a_log, ssd_dt_bias, ssd_d, ssd_norm, w_branch_gla, w_branch_gdn, w_branch_ssd, b_merge, w_out, w_router, b_router, w_gate_up, b_gate_up, w_down, b_down):
    bsz, seq, d = x.shape
    depth = w_mod.shape[0]
    n = bsz * seq
    tm = min(512, seq)
    mod = _modulation(c, w_mod, b_mod)
    x2 = x.reshape(n, d)
    for l in range(depth):
        sh_m, sc_m, g_m, sh_f, sc_f, g_f = [mod[l, :, i * d:(i + 1) * d].reshape(bsz, 1, d)
                                            for i in range(6)]
        w_gla, w_gdn, w_gdn_t, w_ssd, w_ssd_t, w_mrg = _split_w_in(w_in[l])
        gain = norm_mix[l].reshape(1, d)
        proj = functools.partial(_inproj, x2, sc_m, sh_m, gain, seq=seq, tm=tm)
        p_gla = proj(w_gla, None)
        p_gdn, abt = proj(w_gdn, w_gdn_t)
        p_ssd, dtt = proj(w_ssd, w_ssd_t)
        p_mrg = proj(w_mrg, None)
        w2 = jnp.pad(gla_w_gate2[l], ((0, LANES - GLA_GATE_RANK), (0, 0))).astype(BF16)
        tile = min(MIXER_TILE, seq)
        y_gla = _gla(p_gla, w2, gla_b_gate2[l].reshape(1, GLA_QK), gla_norm[l].reshape(1, GLA_DV),
                     bsz=bsz, seq=seq, tile=tile)
        y_gdn = _gdn(p_gdn, abt, gdn_conv_w[l], gdn_a_log[l], gdn_dt_bias[l], gdn_norm[l],
                     bsz=bsz, seq=seq, tile=tile)
        y_ssd = _ssd(p_ssd, dtt, ssd_conv_w[l], ssd_conv_b[l], ssd_a_log[l],
                     ssd_dt_bias[l], ssd_d[l], ssd_norm[l], bsz=bsz, seq=seq, tile=tile)
        x2 = _merge(x2, y_gla, y_gdn, y_ssd, p_mrg, b_merge[l],
                    w_branch_gla[l].astype(BF16), w_branch_gdn[l].astype(BF16),
                    w_branch_ssd[l].astype(BF16), w_out[l].astype(BF16), g_m, seq=seq, tm=tm)
        x2 = _moe(x2, sc_f, sh_f, norm_ffn[l].reshape(1, d), g_f, w_router[l], b_router[l],
                  w_gate_up[l], b_gate_up[l], w_down[l], b_down[l], seq=seq, tm=tm)
    return _final_norm(x2, norm_final, tm=tm).reshape(bsz, seq, d)
```

```python
import functools

import jax
import jax.numpy as jnp
from jax import lax
from jax.experimental import pallas as pl
from jax.experimental.pallas import tpu as pltpu

F32 = jnp.float32
BF16 = jnp.bfloat16
I32 = jnp.int32
HIGHEST = lax.Precision.HIGHEST

CHUNK = 64
NORM_EPS = 1e-6
GLA_HEADS, GLA_DK, GLA_DV, GLA_GATE_RANK, GLA_GATE_TAU = 4, 64, 128, 16, 16.0
GDN_HEADS, GDN_DK, GDN_DV, GDN_CONV = 4, 128, 128, 4
SSD_HEADS, SSD_HEAD_DIM, SSD_GROUPS, SSD_STATE, SSD_CONV = 8, 64, 2, 64, 4
SSD_INNER = SSD_HEADS * SSD_HEAD_DIM
N_BRANCH = 3
N_EXPERTS, TOP_K = 32, 4
SWIGLU_LIMIT, SWIGLU_ALPHA = 7.0, 1.702
MOE_BLOCK = 256

LANES = 128
SUBLANES = 8
DMA_GROUP = 8
DISPATCH_TILE = 1024
CONV_PAD = 8
GATE_ROWS = 8
MIXER_TILE = 4 * CHUNK
VMEM_LIMIT = 48 * 1024 * 1024


def _params(*sem):
    return pltpu.CompilerParams(dimension_semantics=sem, vmem_limit_bytes=VMEM_LIMIT)


def _mm(a, b):
    return lax.dot_general(a.astype(BF16), b.astype(BF16), (((1,), (0,)), ((), ())),
                           preferred_element_type=F32)


def _mm_nt(a, b):
    return lax.dot_general(a.astype(BF16), b.astype(BF16), (((1,), (1,)), ((), ())),
                           preferred_element_type=F32)


def _mm_tn(a, b):
    return lax.dot_general(a.astype(BF16), b.astype(BF16), (((0,), (0,)), ((), ())),
                           preferred_element_type=F32)


def _mm_f32(a, b):
    return lax.dot_general(a, b, (((1,), (0,)), ((), ())), precision=HIGHEST,
                           preferred_element_type=F32)


def _mm_nt_f32(a, b):
    return lax.dot_general(a, b, (((1,), (1,)), ((), ())), precision=HIGHEST,
                           preferred_element_type=F32)


def _split2(x):
    hi = x.astype(BF16)
    return hi, (x - hi.astype(F32)).astype(BF16)


def _split3(x):
    hi = x.astype(BF16)
    r = x - hi.astype(F32)
    mid = r.astype(BF16)
    return hi, mid, (r - mid.astype(F32)).astype(BF16)


def _mm_split(a, b):
    (a_hi, a_lo), (b_hi, b_lo) = a, b
    m = a_hi.shape[0]
    both = _mm(jnp.concatenate([a_hi, a_lo], axis=0), b_hi)
    return both[:m] + both[m:] + _mm(a_hi, b_lo)


def _sigmoid(x):
    return 1.0 / (1.0 + jnp.exp(-x))


def _silu(x):
    return x * _sigmoid(x)


def _softplus(x):
    return jnp.maximum(x, 0.0) + jnp.log1p(jnp.exp(-jnp.abs(x)))


def _log_sigmoid(x):
    return jnp.minimum(x, 0.0) - jnp.log1p(jnp.exp(-jnp.abs(x)))


def _rms(x):
    return x * lax.rsqrt(jnp.mean(x * x, axis=-1, keepdims=True) + NORM_EPS)


def _norm_mod(x, gain, scale, shift):
    return _rms(x) * gain * (1.0 + scale) + shift


def _chunk_masks():
    row = lax.broadcasted_iota(I32, (CHUNK, CHUNK), 0)
    col = lax.broadcasted_iota(I32, (CHUNK, CHUNK), 1)
    return row >= col, row > col


def _masked_decay(cum_col, cum_row, mask):
    return jnp.where(mask, jnp.exp(jnp.where(mask, cum_col - cum_row, 0.0)), 0.0)


def _mod_kernel(c_ref, w_ref, b_ref, o_ref):
    o_ref[...] = _mm_f32(_silu(c_ref[...]), w_ref[...]) + b_ref[...]


def _modulation(c, w_mod, b_mod):
    depth, d, six_d = w_mod.shape
    bsz = c.shape[0]
    n_col = six_d // d
    return pl.pallas_call(
        _mod_kernel,
        out_shape=jax.ShapeDtypeStruct((depth, bsz, six_d), F32),
        grid=(depth, n_col),
        in_specs=[
            pl.BlockSpec((bsz, d), lambda l, j: (0, 0)),
            pl.BlockSpec((None, d, d), lambda l, j: (l, 0, j)),
            pl.BlockSpec((None, 1, d), lambda l, j: (l, 0, j)),
        ],
        out_specs=pl.BlockSpec((None, bsz, d), lambda l, j: (l, 0, j)),
        compiler_params=_params("arbitrary", "arbitrary"),
        name="adaln_mod",
    )(c, w_mod, b_mod.reshape(depth, 1, six_d))


def _inproj_kernel(x_ref, sc_ref, sh_ref, g_ref, w_ref, *rest, has_t):
    h = _norm_mod(x_ref[...], g_ref[...], sc_ref[...], sh_ref[...]).astype(BF16)
    if has_t:
        wt_ref, o_ref, ot_ref = rest
        ot_ref[...] = _mm_nt(wt_ref[...], h)
    else:
        (o_ref,) = rest
    o_ref[...] = _mm(h, w_ref[...])


def _inproj(x2, scale, shift, gain, w, wt, *, seq, tm):
    n, d = x2.shape
    width = w.shape[1]
    per_batch = seq // tm
    batch_spec = pl.BlockSpec((None, 1, d), lambda i: (i // per_batch, 0, 0))
    in_specs = [
        pl.BlockSpec((tm, d), lambda i: (i, 0)),
        batch_spec, batch_spec,
        pl.BlockSpec((1, d), lambda i: (0, 0)),
        pl.BlockSpec((d, width), lambda i: (0, 0)),
    ]
    args = [x2, scale, shift, gain, w]
    out_shape = [jax.ShapeDtypeStruct((n, width), F32)]
    out_specs = [pl.BlockSpec((tm, width), lambda i: (i, 0))]
    if wt is not None:
        in_specs.append(pl.BlockSpec((GATE_ROWS, d), lambda i: (0, 0)))
        args.append(wt)
        out_shape.append(jax.ShapeDtypeStruct((GATE_ROWS, n), F32))
        out_specs.append(pl.BlockSpec((GATE_ROWS, tm), lambda i: (0, i)))
    outs = pl.pallas_call(
        functools.partial(_inproj_kernel, has_t=wt is not None),
        out_shape=out_shape, grid=(n // tm,), in_specs=in_specs, out_specs=out_specs,
        compiler_params=_params("arbitrary"),
        name="inproj",
    )(*args)
    return outs if wt is not None else outs[0]


GLA_QK = GLA_HEADS * GLA_DK
GLA_V = GLA_HEADS * GLA_DV
GLA_WIDTH = 2 * GLA_QK + 2 * GLA_V + LANES


def _chunk_cumsum_matrices(tile):
    r = jnp.arange(tile)
    tri = ((r[:, None] // CHUNK == r[None, :] // CHUNK) & (r[:, None] >= r[None, :])).astype(BF16)
    tri3 = jnp.concatenate([tri, tri, tri], axis=1)
    return tri3, tri3.T


def _gla_kernel(p_ref, tri_ref, w2_ref, b2_ref, ng_ref, o_ref, st_ref, oin_ref, qd_ref):
    tile = p_ref.shape[0]
    heads = range(GLA_HEADS)

    @pl.when(pl.program_id(1) == 0)
    def _():
        st_ref[...] = jnp.zeros_like(st_ref)

    incl, _ = _chunk_masks()
    lr = p_ref[:, 2 * GLA_QK + 2 * GLA_V:GLA_WIDTH]
    gk = _log_sigmoid(_mm(lr, w2_ref[...]) + b2_ref[...]) / GLA_GATE_TAU
    bcum = _mm(tri_ref[...], jnp.concatenate(_split3(gk), axis=0))
    ks = [slice(h * GLA_DK, (h + 1) * GLA_DK) for h in heads]
    vs = [slice(h * GLA_DV, (h + 1) * GLA_DV) for h in heads]

    upds, decays = [], []
    for c in range(tile // CHUNK):
        rows = slice(c * CHUNK, (c + 1) * CHUNK)
        b = bcum[rows]
        b_last = b[CHUNK - 1:CHUNK, :]
        q_dec = (p_ref[rows, 0:GLA_QK] * (GLA_DK ** -0.5) * jnp.exp(b)).astype(BF16)
        k = p_ref[rows, GLA_QK:2 * GLA_QK]
        k_neg = (k * jnp.exp(-b)).astype(BF16)
        k_pos = (k * jnp.exp(b_last - b)).astype(BF16)
        v = [p_ref[rows, 2 * GLA_QK + h * GLA_DV:2 * GLA_QK + (h + 1) * GLA_DV].astype(BF16)
             for h in heads]
        atts = [jnp.where(incl, _mm_nt(q_dec[:, ks[h]], k_neg[:, ks[h]]), 0.0) for h in heads]
        for h in heads:
            oin_ref[rows, vs[h]] = _mm(atts[h], v[h])
        upds.append([_mm_tn(v[h], k_pos[:, ks[h]]) for h in heads])
        decays.append(jnp.exp(b_last))
        qd_ref[rows, :] = q_dec

    for c in range(tile // CHUNK):
        rows = slice(c * CHUNK, (c + 1) * CHUNK)
        sts = [st_ref[h] for h in heads]
        os = [oin_ref[rows, vs[h]] + _mm_nt(qd_ref[rows, ks[h]], sts[h]) for h in heads]
        for h in heads:
            st_ref[h] = sts[h] * decays[c][:, ks[h]] + upds[c][h]
        for h in heads:
            rs = slice(2 * GLA_QK + GLA_V + h * GLA_DV, 2 * GLA_QK + GLA_V + (h + 1) * GLA_DV)
            o = _rms(os[h]) * ng_ref[...]
            o_ref[rows, vs[h]] = (o * _silu(p_ref[rows, rs])).astype(o_ref.dtype)


def _gla(p, w2, b2, ng, *, bsz, seq, tile):
    nt = seq // tile
    tri3, _ = _chunk_cumsum_matrices(tile)
    const = lambda shape: pl.BlockSpec(shape, lambda b, c: (0,) * len(shape))
    return pl.pallas_call(
        _gla_kernel,
        out_shape=jax.ShapeDtypeStruct((bsz * seq, GLA_V), BF16),
        grid=(bsz, nt),
        in_specs=[
            pl.BlockSpec((tile, GLA_WIDTH), lambda b, c: (b * nt + c, 0)),
            const(tri3.shape), const((LANES, GLA_QK)), const((1, GLA_QK)), const((1, GLA_DV)),
        ],
        out_specs=pl.BlockSpec((tile, GLA_V), lambda b, c: (b * nt + c, 0)),
        scratch_shapes=[
            pltpu.VMEM((GLA_HEADS, GLA_DV, GLA_DK), F32),
            pltpu.VMEM((tile, GLA_V), F32),
            pltpu.VMEM((tile, GLA_QK), BF16),
        ],
        compiler_params=_params("arbitrary", "arbitrary"),
        name="gla_mixer",
    )(p, tri3, w2, b2, ng)


def _conv_window(ext_ref, w_ref, r0, taps):
    out = None
    for j in range(taps):
        start = CONV_PAD + r0 - (taps - 1) + j
        term = ext_ref[start:start + CHUNK, :] * w_ref[j:j + 1, :]
        out = term if out is None else out + term
    return out


SSD_BC = SSD_GROUPS * SSD_STATE
SSD_XBC = SSD_INNER + 2 * SSD_BC
SSD_WIDTH = SSD_INNER + SSD_XBC + LANES
SSD_HG = SSD_HEADS // SSD_GROUPS


def _ssd_kernel(p_ref, dtt_ref, tri_ref, triu_ref, expand_ref, cw_ref, cb_ref, alog_e_ref, dtb_ref,
                alog_c_ref, dtb_c_ref, dskip_ref, ng_ref, o_ref, ext_ref, st_ref, y_ref, xs_ref, cm_ref):
    tile = p_ref.shape[0]
    heads = range(SSD_HEADS)
    groups = range(SSD_GROUPS)
    gw = SSD_INNER // SSD_GROUPS

    @pl.when(pl.program_id(1) == 0)
    def _():
        st_ref[...] = jnp.zeros_like(st_ref)
        ext_ref[0:CONV_PAD, :] = jnp.zeros((CONV_PAD, SSD_XBC), F32)

    incl, _ = _chunk_masks()
    ext_ref[CONV_PAD:CONV_PAD + tile, :] = p_ref[:, SSD_INNER:SSD_INNER + SSD_XBC]
    dt_col = _softplus(p_ref[:, SSD_INNER + SSD_XBC:SSD_WIDTH] + dtb_ref[...])
    dt_e = _mm(jnp.concatenate(_split3(dt_col), axis=1), expand_ref[...])
    cum_e = _mm(tri_ref[...], jnp.concatenate(_split3(dt_e * (-jnp.exp(alog_e_ref[...]))), axis=0))
    dt_row = _softplus(dtt_ref[...] + dtb_c_ref[...])
    cum_row = _mm(jnp.concatenate(_split3(dt_row * (-jnp.exp(alog_c_ref[...]))), axis=1),
                  triu_ref[...])
    hs = [slice(h * SSD_HEAD_DIM, (h + 1) * SSD_HEAD_DIM) for h in heads]
    gs = [slice(g * gw, (g + 1) * gw) for g in groups]

    upds, st_decays = [], []
    for c in range(tile // CHUNK):
        r0 = c * CHUNK
        rows = slice(r0, r0 + CHUNK)
        xbc = _silu(_conv_window(ext_ref, cw_ref, r0, SSD_CONV) + cb_ref[...])
        xs = xbc[:, 0:SSD_INNER]
        xs_ref[rows, :] = xs
        cm_ref[rows, :] = xbc[:, SSD_INNER + SSD_BC:SSD_XBC].astype(BF16)
        bm = [xbc[:, SSD_INNER + g * SSD_STATE:SSD_INNER + (g + 1) * SSD_STATE] for g in groups]
        cm = [xbc[:, SSD_INNER + SSD_BC + g * SSD_STATE:SSD_INNER + SSD_BC + (g + 1) * SSD_STATE]
              for g in groups]
        ce = cum_e[rows]
        cle = ce[CHUNK - 1:CHUNK, :]
        xdt = xs * dt_e[rows]
        xdt_bf = xdt.astype(BF16)
        xdec = (xdt * jnp.exp(cle - ce)).astype(BF16)
        cb = [_mm_nt(cm[g], bm[g]) for g in groups]
        segs = [_masked_decay(ce[:, hs[h]], cum_row[h:h + 1, rows], incl) for h in heads]
        for h in heads:
            y_ref[rows, hs[h]] = _mm(cb[h // SSD_HG] * segs[h], xdt_bf[:, hs[h]])
        upds.append([_mm_tn(bm[g], xdec[:, gs[g]]) for g in groups])
        st_decays.append(jnp.exp(cle))
    ext_ref[0:CONV_PAD, :] = ext_ref[tile:tile + CONV_PAD, :]

    for c in range(tile // CHUNK):
        rows = slice(c * CHUNK, (c + 1) * CHUNK)
        sts = [st_ref[g] for g in groups]
        y_in = [_mm(cm_ref[rows, g * SSD_STATE:(g + 1) * SSD_STATE], sts[g]) for g in groups]
        for g in groups:
            st_ref[g] = st_decays[c][:, gs[g]] * sts[g] + upds[c][g]
        y = y_ref[rows, :] + jnp.concatenate(y_in, axis=1) * jnp.exp(cum_e[rows])
        y = (y + dskip_ref[...] * xs_ref[rows, :]) * _silu(p_ref[rows, 0:SSD_INNER])
        for g in groups:
            o_ref[rows, gs[g]] = (_rms(y[:, gs[g]]) * ng_ref[:, gs[g]]).astype(o_ref.dtype)


def _ssd(p, dtt, conv_w, conv_b, a_log, dt_bias, d_skip, ng, *, bsz, seq, tile):
    nt = seq // tile
    pad = lambda v: jnp.pad(v, (0, LANES - v.shape[0])).reshape(1, LANES)
    col = lambda v: v.reshape(SSD_HEADS, 1)
    const = lambda shape: pl.BlockSpec(shape, lambda b, c: (0,) * len(shape))
    tri3, triu3 = _chunk_cumsum_matrices(tile)
    per_channel = lambda v: jnp.repeat(v, SSD_HEAD_DIM).reshape(1, SSD_INNER)
    expand = (jnp.arange(LANES)[:, None] == jnp.arange(SSD_INNER)[None, :] // SSD_HEAD_DIM)
    expand3 = jnp.concatenate([expand.astype(BF16)] * 3, axis=0)
    return pl.pallas_call(
        _ssd_kernel,
        out_shape=jax.ShapeDtypeStruct((bsz * seq, SSD_INNER), BF16),
        grid=(bsz, nt),
        in_specs=[
            pl.BlockSpec((tile, SSD_WIDTH), lambda b, c: (b * nt + c, 0)),
            pl.BlockSpec((GATE_ROWS, tile), lambda b, c: (0, b * nt + c)),
            const(tri3.shape), const(triu3.shape), const(expand3.shape),
            const((SSD_CONV, SSD_XBC)), const((1, SSD_XBC)),
            const((1, SSD_INNER)), const((1, LANES)),
            const((SSD_HEADS, 1)), const((SSD_HEADS, 1)),
            const((1, SSD_INNER)), const((1, SSD_INNER)),
        ],
        out_specs=pl.BlockSpec((tile, SSD_INNER), lambda b, c: (b * nt + c, 0)),
        scratch_shapes=[
            pltpu.VMEM((CONV_PAD + tile, SSD_XBC), F32),
            pltpu.VMEM((SSD_GROUPS, SSD_STATE, SSD_INNER // SSD_GROUPS), F32),
            pltpu.VMEM((tile, SSD_INNER), F32),
            pltpu.VMEM((tile, SSD_INNER), F32),
            pltpu.VMEM((tile, SSD_BC), BF16),
        ],
        compiler_params=_params("arbitrary", "arbitrary"),
        name="ssd_mixer",
    )(p, dtt, tri3, triu3, expand3, conv_w, conv_b.reshape(1, SSD_XBC), per_channel(a_log),
      pad(dt_bias), col(a_log), col(dt_bias), per_channel(d_skip), ng.reshape(1, SSD_INNER))


GDN_QK = GDN_HEADS * GDN_DK
GDN_V = GDN_HEADS * GDN_DV
GDN_QKV = 2 * GDN_QK + GDN_V
GDN_WIDTH = GDN_QKV + GDN_V + LANES


def _l2norm(x):
    return x * lax.rsqrt(jnp.sum(x * x, axis=-1, keepdims=True) + 1e-6)


def _unit_lower_inverses(mats):
    shape = mats[0].shape
    eye = jnp.where(lax.broadcasted_iota(I32, shape, 0) == lax.broadcasted_iota(I32, shape, 1),
                    1.0, 0.0)
    ps = [eye - a for a in mats]
    n_splits = [_split2(-a) for a in mats]
    steps = (shape[0] - 1).bit_length() - 1
    for _ in range(steps):
        n_splits = [_split2(_mm_split(ns, ns)) for ns in n_splits]
        ps = [p + _mm_split(_split2(p), ns) for p, ns in zip(ps, n_splits)]
    return ps


def _gdn_kernel(p_ref, abt_ref, tri_ref, triu_ref, cw_ref, alog_ref, dtb_ref, alog_c_ref, dtb_c_ref,
                ng_ref, o_ref, ext_ref, st_ref, upre_ref, wmix_ref, qdec_ref, kdec_ref, pmat_ref):
    tile = p_ref.shape[0]

    @pl.when(pl.program_id(1) == 0)
    def _():
        st_ref[...] = jnp.zeros_like(st_ref)
        ext_ref[0:CONV_PAD, :] = jnp.zeros((CONV_PAD, GDN_QKV), F32)

    incl, strict = _chunk_masks()
    ext_ref[CONV_PAD:CONV_PAD + tile, :] = p_ref[:, 0:GDN_QKV]
    ab = p_ref[:, GDN_QKV + GDN_V:GDN_WIDTH]
    beta_all = _sigmoid(ab)
    g_col = -jnp.exp(alog_ref[...]) * _softplus(ab + dtb_ref[...])
    cum_col = _mm(tri_ref[...], jnp.concatenate(_split3(g_col), axis=0))
    g_row = -jnp.exp(alog_c_ref[...]) * _softplus(abt_ref[...] + dtb_c_ref[...])
    cum_row = _mm(jnp.concatenate(_split3(g_row), axis=1), triu_ref[...])

    a_mats, rhss, where = [], [], []
    for c in range(tile // CHUNK):
        r0 = c * CHUNK
        rows = slice(r0, r0 + CHUNK)
        qkv = _silu(_conv_window(ext_ref, cw_ref, r0, GDN_CONV))
        for h in range(GDN_HEADS):
            hs = slice(h * GDN_DV, (h + 1) * GDN_DV)
            q = _l2norm(qkv[:, h * GDN_DK:(h + 1) * GDN_DK]) * (GDN_DK ** -0.5)
            k = _l2norm(qkv[:, GDN_QK + h * GDN_DK:GDN_QK + (h + 1) * GDN_DK])
            v = qkv[:, 2 * GDN_QK + h * GDN_DV:2 * GDN_QK + (h + 1) * GDN_DV]
            beta = beta_all[rows, GDN_HEADS + h:GDN_HEADS + h + 1]
            cc = cum_col[rows, h:h + 1]
            cl = cum_col[r0 + CHUNK - 1:r0 + CHUNK, h:h + 1]
            gam = _masked_decay(cc, cum_row[h:h + 1, rows], incl)
            a_mats.append(jnp.where(strict, beta * _mm_nt(k, k) * gam, 0.0))
            rhss.append(_split2(jnp.concatenate([beta * v, (beta * jnp.exp(cc)) * k], axis=1)))
            where.append((rows, hs))
            qdec_ref[rows, hs] = (q * jnp.exp(cc)).astype(BF16)
            kdec_ref[rows, hs] = (k * jnp.exp(cl - cc)).astype(BF16)
            pmat_ref[rows, h * CHUNK:(h + 1) * CHUNK] = (_mm_nt(q, k) * gam).astype(BF16)
    ext_ref[0:CONV_PAD, :] = ext_ref[tile:tile + CONV_PAD, :]
    t_invs = _unit_lower_inverses(a_mats)
    for t_inv, rhs, (rows, hs) in zip(t_invs, rhss, where):
        sol = _mm_split(_split2(t_inv), rhs)
        upre_ref[rows, hs] = sol[:, 0:GDN_DV]
        wmix_ref[rows, hs] = sol[:, GDN_DV:2 * GDN_DV].astype(BF16)

    heads = range(GDN_HEADS)
    for c in range(tile // CHUNK):
        r0 = c * CHUNK
        rows = slice(r0, r0 + CHUNK)
        hss = [slice(h * GDN_DV, (h + 1) * GDN_DV) for h in heads]
        ms = [st_ref[h] for h in heads]
        us = [upre_ref[rows, hss[h]] - _mm(wmix_ref[rows, hss[h]], ms[h]) for h in heads]
        os = [_mm(qdec_ref[rows, hss[h]], ms[h])
              + _mm(pmat_ref[rows, h * CHUNK:(h + 1) * CHUNK], us[h]) for h in heads]
        for h in heads:
            cl = cum_col[r0 + CHUNK - 1:r0 + CHUNK, h:h + 1]
            st_ref[h] = jnp.exp(cl) * ms[h] + _mm_tn(kdec_ref[rows, hss[h]], us[h])
        for h in heads:
            o = _rms(os[h]) * ng_ref[...]
            gs = slice(GDN_QKV + h * GDN_DV, GDN_QKV + (h + 1) * GDN_DV)
            o_ref[rows, hss[h]] = (o * _silu(p_ref[rows, gs])).astype(o_ref.dtype)


def _gdn(p, abt, conv_w, a_log, dt_bias, ng, *, bsz, seq, tile):
    nt = seq // tile
    pad = lambda v: jnp.pad(v, (0, LANES - v.shape[0])).reshape(1, LANES)
    col = lambda v: jnp.pad(v, (0, GATE_ROWS - v.shape[0])).reshape(GATE_ROWS, 1)
    const = lambda shape: pl.BlockSpec(shape, lambda b, c: (0,) * len(shape))
    tri3, triu3 = _chunk_cumsum_matrices(tile)
    return pl.pallas_call(
        _gdn_kernel,
        out_shape=jax.ShapeDtypeStruct((bsz * seq, GDN_V), BF16),
        grid=(bsz, nt),
        in_specs=[
            pl.BlockSpec((tile, GDN_WIDTH), lambda b, c: (b * nt + c, 0)),
            pl.BlockSpec((GATE_ROWS, tile), lambda b, c: (0, b * nt + c)),
            const(tri3.shape), const(triu3.shape),
            const((GDN_CONV, GDN_QKV)),
            const((1, LANES)), const((1, LANES)),
            const((GATE_ROWS, 1)), const((GATE_ROWS, 1)),
            const((1, GDN_DV)),
        ],
        out_specs=pl.BlockSpec((tile, GDN_V), lambda b, c: (b * nt + c, 0)),
        scratch_shapes=[
            pltpu.VMEM((CONV_PAD + tile, GDN_QKV), F32),
            pltpu.VMEM((GDN_HEADS, GDN_DK, GDN_DV), F32),
            pltpu.VMEM((tile, GDN_V), F32),
            pltpu.VMEM((tile, GDN_V), BF16),
            pltpu.VMEM((tile, GDN_QK), BF16),
            pltpu.VMEM((tile, GDN_QK), BF16),
            pltpu.VMEM((tile, GDN_HEADS * CHUNK), BF16),
        ],
        compiler_params=_params("arbitrary", "arbitrary"),
        name="gdn_mixer",
    )(p, abt, tri3, triu3, conv_w, pad(a_log), pad(dt_bias), col(a_log), col(dt_bias),
      ng.reshape(1, GDN_DV))


def _merge_kernel(x_ref, ya_ref, yb_ref, yc_ref, mr_ref, bm_ref, wa_ref, wb_ref, wc_ref, wo_ref,
                  gm_ref, o_ref):
    d = x_ref.shape[1]
    merged = None
    for i, (y_ref, w_ref) in enumerate(((ya_ref, wa_ref), (yb_ref, wb_ref), (yc_ref, wc_ref))):
        gate = _sigmoid(mr_ref[:, i * d:(i + 1) * d] + bm_ref[:, i * d:(i + 1) * d])
        term = gate * _mm(y_ref[...], w_ref[...])
        merged = term if merged is None else merged + term
    o_ref[...] = x_ref[...] + gm_ref[...] * _mm(merged, wo_ref[...])


def _merge(x2, ya, yb, yc, mr, b_merge, wa, wb, wc, wo, gm, *, seq, tm):
    n, d = x2.shape
    per_batch = seq // tm
    row = lambda w: pl.BlockSpec((tm, w), lambda i: (i, 0))
    const = lambda shape: pl.BlockSpec(shape, lambda i: (0,) * len(shape))
    return pl.pallas_call(
        _merge_kernel,
        out_shape=jax.ShapeDtypeStruct((n, d), F32),
        grid=(n // tm,),
        in_specs=[
            row(d), row(ya.shape[1]), row(yb.shape[1]), row(yc.shape[1]), row(N_BRANCH * d),
            const((1, N_BRANCH * d)),
            const(wa.shape), const(wb.shape), const(wc.shape), const(wo.shape),
            pl.BlockSpec((None, 1, d), lambda i: (i // per_batch, 0, 0)),
        ],
        out_specs=row(d),
        compiler_params=_params("arbitrary"),
        name="merge_out",
    )(x2, ya, yb, yc, mr, b_merge.reshape(1, N_BRANCH * d), wa, wb, wc, wo, gm)


def _router_kernel(x_ref, sc_ref, sh_ref, g_ref, wr_ref, br_ref,
                   h_ref, idx_ref, gate_ref, rank_ref, cnt_ref, carry_ref):
    @pl.when(pl.program_id(0) == 0)
    def _():
        carry_ref[...] = jnp.zeros_like(carry_ref)

    tm = x_ref.shape[0]
    h = _norm_mod(x_ref[...], g_ref[...], sc_ref[...], sh_ref[...])
    _store_token_tiles(h_ref, h)
    logits = _mm_nt_f32(wr_ref[...], h) + br_ref[...]
    e_iota = lax.broadcasted_iota(I32, logits.shape, 0).astype(F32)
    vals, idxs, sels = [], [], []
    cur = logits
    for _ in range(TOP_K):
        m = jnp.max(cur, axis=0, keepdims=True)
        idx = jnp.min(jnp.where(cur == m, e_iota, float(N_EXPERTS)), axis=0, keepdims=True)
        sel = e_iota == idx
        cur = jnp.where(sel, -jnp.inf, cur)
        vals.append(m)
        idxs.append(idx.astype(I32))
        sels.append(sel)
    exps = [jnp.exp(v - vals[0]) for v in vals]
    denom = exps[0] + exps[1] + exps[2] + exps[3]
    onehot = jnp.zeros(logits.shape, F32)
    for sel in sels:
        onehot = onehot + jnp.where(sel, 1.0, 0.0)
    row = lax.broadcasted_iota(I32, (tm, tm), 0)
    col = lax.broadcasted_iota(I32, (tm, tm), 1)
    before = _mm(onehot, jnp.where(row < col, 1.0, 0.0)) + carry_ref[...]
    zeros_i = jnp.zeros((GATE_ROWS - TOP_K, tm), I32)
    idx_ref[...] = jnp.concatenate(idxs + [zeros_i], axis=0)
    gate_ref[...] = jnp.concatenate([e / denom for e in exps] + [zeros_i.astype(F32)], axis=0)
    ranks = [jnp.sum(jnp.where(sel, before, 0.0), axis=0, keepdims=True).astype(I32) for sel in sels]
    rank_ref[...] = jnp.concatenate(ranks + [zeros_i], axis=0)
    carry_ref[...] = carry_ref[...] + jnp.sum(onehot, axis=1, keepdims=True)
    cnt_ref[...] = jnp.broadcast_to(carry_ref[...], cnt_ref.shape).astype(I32)


def _router(x2, scale, shift, gain, w_router_t, b_router, *, seq, tm):
    n, d = x2.shape
    per_batch = seq // tm
    batch_spec = pl.BlockSpec((None, 1, d), lambda i: (i // per_batch, 0, 0))
    lane_out = pl.BlockSpec((GATE_ROWS, tm), lambda i: (0, i))
    return pl.pallas_call(
        _router_kernel,
        out_shape=[
            jax.ShapeDtypeStruct((n * SUBLANES, LANES), F32),
            jax.ShapeDtypeStruct((GATE_ROWS, n), I32),
            jax.ShapeDtypeStruct((GATE_ROWS, n), F32),
            jax.ShapeDtypeStruct((GATE_ROWS, n), I32),
            jax.ShapeDtypeStruct((N_EXPERTS, LANES), I32),
        ],
        grid=(n // tm,),
        in_specs=[
            pl.BlockSpec((tm, d), lambda i: (i, 0)),
            batch_spec, batch_spec,
            pl.BlockSpec((1, d), lambda i: (0, 0)),
            pl.BlockSpec((N_EXPERTS, d), lambda i: (0, 0)),
            pl.BlockSpec((N_EXPERTS, 1), lambda i: (0, 0)),
        ],
        out_specs=[
            pl.BlockSpec((tm * SUBLANES, LANES), lambda i: (i, 0)),
            lane_out, lane_out, lane_out,
            pl.BlockSpec((N_EXPERTS, LANES), lambda i: (0, 0)),
        ],
        scratch_shapes=[pltpu.VMEM((N_EXPERTS, 1), F32)],
        compiler_params=_params("arbitrary"),
        name="router",
    )(x2, scale, shift, gain, w_router_t, b_router.reshape(N_EXPERTS, 1))


def _store_token_tiles(ref, value):
    rows = value.shape[0]
    for s in range(SUBLANES):
        ref[pl.ds(s, rows, stride=SUBLANES), :] = value[:, s * LANES:(s + 1) * LANES]


def _load_token_tiles(ref, rows):
    return jnp.concatenate([ref[pl.ds(s, rows, stride=SUBLANES), :] for s in range(SUBLANES)],
                           axis=1)


def _tile_copy(src, src_row, dst, dst_row, sem):
    src_at = pl.multiple_of(src_row * SUBLANES, SUBLANES)
    dst_at = pl.multiple_of(dst_row * SUBLANES, SUBLANES)
    return pltpu.make_async_copy(src.at[pl.ds(src_at, SUBLANES), :],
                                 dst.at[pl.ds(dst_at, SUBLANES), :], sem)


def _load_positions(pos_hbm, pos_smem, sem, count):
    start = pl.multiple_of(pl.program_id(0) * count, count)
    cp = pltpu.make_async_copy(pos_hbm.at[pl.ds(start, count)], pos_smem, sem)
    cp.start()
    cp.wait()


def _dispatch_kernel(meta_ref, pos_hbm, h_hbm, xs_hbm, pos_smem, zero_ref, pos_sem, row_sem):
    td = pos_smem.shape[0] // TOP_K
    n_blocks = xs_hbm.shape[0] // (MOE_BLOCK * SUBLANES)
    block_rows = MOE_BLOCK * SUBLANES

    @pl.when(pl.program_id(0) == 0)
    def _():
        zero_ref[...] = jnp.zeros_like(zero_ref)

        def fill(block):
            at = pl.multiple_of(block * block_rows, block_rows)
            return pltpu.make_async_copy(zero_ref, xs_hbm.at[pl.ds(at, block_rows), :], row_sem)

        def each_fill(act):
            for e in range(N_EXPERTS):
                @pl.when(meta_ref[e] > 0)
                def _():
                    act(fill(meta_ref[N_EXPERTS + e] // MOE_BLOCK - 1))

                @pl.when(meta_ref[2 * N_EXPERTS] + e < n_blocks)
                def _():
                    act(fill(meta_ref[2 * N_EXPERTS] + e))

        each_fill(lambda cp: cp.start())
        each_fill(lambda cp: cp.wait())

    _load_positions(pos_hbm, pos_smem, pos_sem, TOP_K * td)
    tok0 = pl.program_id(0) * td

    def start(g, carry):
        t0 = g * DMA_GROUP
        slots = [pos_smem[TOP_K * t0 + j] for j in range(TOP_K * DMA_GROUP)]
        for j, slot in enumerate(slots):
            _tile_copy(h_hbm, tok0 + t0 + j // TOP_K, xs_hbm, slot, row_sem).start(priority=j % 2)
        return carry

    def wait(g, carry):
        for _ in range(TOP_K * DMA_GROUP):
            _tile_copy(h_hbm, 0, xs_hbm, 0, row_sem).wait()
        return carry

    lax.fori_loop(0, td // DMA_GROUP, start, 0)
    lax.fori_loop(0, td // DMA_GROUP, wait, 0)


def _dispatch(meta, pos_flat, h3, n_rows, *, td):
    n = h3.shape[0] // SUBLANES
    grid_spec = pltpu.PrefetchScalarGridSpec(
        num_scalar_prefetch=1,
        grid=(n // td,),
        in_specs=[pl.BlockSpec(memory_space=pl.ANY), pl.BlockSpec(memory_space=pl.ANY)],
        out_specs=pl.BlockSpec(memory_space=pl.ANY),
        scratch_shapes=[
            pltpu.SMEM((TOP_K * td,), I32),
            pltpu.VMEM((MOE_BLOCK * SUBLANES, LANES), F32),
            pltpu.SemaphoreType.DMA,
            pltpu.SemaphoreType.DMA,
        ],
    )
    return pl.pallas_call(
        _dispatch_kernel,
        out_shape=jax.ShapeDtypeStruct((n_rows * SUBLANES, LANES), F32),
        grid_spec=grid_spec,
        compiler_params=_params("arbitrary"),
        name="moe_dispatch",
    )(meta, pos_flat, h3)


PAIR_TILE = 2 * LANES


def _pair_regroup_kernel(w_ref, o_ref):
    r = lax.broadcasted_iota(I32, (PAIR_TILE, PAIR_TILE), 0)
    c = lax.broadcasted_iota(I32, (PAIR_TILE, PAIR_TILE), 1)
    src = jnp.where(c < LANES, 2 * c, 2 * (c - LANES) + 1)
    perm = jnp.where(r == src, 1.0, 0.0).astype(BF16)
    for t in range(w_ref.shape[1] // PAIR_TILE):
        cols = slice(t * PAIR_TILE, (t + 1) * PAIR_TILE)
        o_ref[:, cols] = _mm(w_ref[:, cols], perm).astype(o_ref.dtype)


def _pair_regroup(w_all, layer, *, tr):
    _, e, d, width = w_all.shape
    return pl.pallas_call(
        _pair_regroup_kernel,
        out_shape=jax.ShapeDtypeStruct((e, d, width), BF16),
        grid=(e, d // tr),
        in_specs=[pl.BlockSpec((None, None, tr, width), lambda i, j: (layer, i, j, 0))],
        out_specs=pl.BlockSpec((None, tr, width), lambda i, j: (i, j, 0)),
        compiler_params=_params("arbitrary", "arbitrary"),
        name="gate_up_regroup",
    )(w_all)


def _expert_kernel(be_ref, x_ref, wgu_ref, bgu_ref, wd_ref, bd_ref, o_ref):
    del be_ref
    f = wd_ref.shape[0]
    x = _load_token_tiles(x_ref, MOE_BLOCK)
    gu = _mm(x, wgu_ref[...]) + bgu_ref[...]
    acts = []
    for t in range(f // LANES):
        gate = jnp.minimum(gu[:, t * PAIR_TILE:t * PAIR_TILE + LANES], SWIGLU_LIMIT)
        up = jnp.clip(gu[:, t * PAIR_TILE + LANES:(t + 1) * PAIR_TILE], -SWIGLU_LIMIT, SWIGLU_LIMIT)
        acts.append(((up + 1.0) * gate * _sigmoid(SWIGLU_ALPHA * gate)).astype(BF16))
    _store_token_tiles(o_ref, _mm(jnp.concatenate(acts, axis=1), wd_ref[...]) + bd_ref[...])


def _experts(block_e, xs, wgu, bgu, wd_all, bd, layer):
    f, d = wd_all.shape[2:]
    block_rows = MOE_BLOCK * SUBLANES
    n_blocks = xs.shape[0] // block_rows
    grid_spec = pltpu.PrefetchScalarGridSpec(
        num_scalar_prefetch=1,
        grid=(n_blocks,),
        in_specs=[
            pl.BlockSpec((block_rows, LANES), lambda i, be: (i, 0)),
            pl.BlockSpec((None, d, 2 * f), lambda i, be: (be[i], 0, 0)),
            pl.BlockSpec((None, 1, 2 * f), lambda i, be: (be[i], 0, 0)),
            pl.BlockSpec((None, None, f, d), lambda i, be: (layer, be[i], 0, 0)),
            pl.BlockSpec((None, 1, d), lambda i, be: (be[i], 0, 0)),
        ],
        out_specs=pl.BlockSpec((block_rows, LANES), lambda i, be: (i, 0)),
    )
    return pl.pallas_call(
        _expert_kernel,
        out_shape=jax.ShapeDtypeStruct(xs.shape, F32),
        grid_spec=grid_spec,
        compiler_params=_params("arbitrary"),
        name="moe_experts",
    )(block_e, xs, wgu, bgu, wd_all, bd)


def _combine_kernel(pos_hbm, ys_hbm, gate_ref, x_ref, gf_ref, *rest, final_norm):
    if final_norm:
        ng_ref, o_ref, buf_ref, pos_smem, pos_sem, row_sem = rest
    else:
        o_ref, buf_ref, pos_smem, pos_sem, row_sem = rest
    tc = x_ref.shape[0]
    _load_positions(pos_hbm, pos_smem, pos_sem, TOP_K * tc)

    def start(g, carry):
        t0 = g * DMA_GROUP
        slots = [pos_smem[TOP_K * t0 + j] for j in range(TOP_K * DMA_GROUP)]
        for j, slot in enumerate(slots):
            _tile_copy(ys_hbm, slot, buf_ref.at[j % TOP_K], t0 + j // TOP_K,
                       row_sem).start(priority=j % 2)
        return carry

    def wait(g, carry):
        for _ in range(TOP_K * DMA_GROUP):
            _tile_copy(ys_hbm, 0, buf_ref.at[0], 0, row_sem).wait()
        return carry

    lax.fori_loop(0, tc // DMA_GROUP, start, 0)
    lax.fori_loop(0, tc // DMA_GROUP, wait, 0)
    gates = [jnp.broadcast_to(gate_ref[:, k:k + 1], (tc, LANES)) for k in range(TOP_K)]
    ys = []
    for s in range(SUBLANES):
        y = None
        for k in range(TOP_K):
            term = gates[k] * buf_ref[k, pl.ds(s, tc, stride=SUBLANES), :]
            y = term if y is None else y + term
        ys.append(y)
    out = x_ref[...] + gf_ref[...] * jnp.concatenate(ys, axis=1)
    if final_norm:
        out = _rms(out) * ng_ref[...]
    o_ref[...] = out


def _combine(pos_flat, ys, gates, x2, gf, final_gain, *, seq, tc):
    n, d = x2.shape
    per_batch = seq // tc
    final_norm = final_gain is not None
    in_specs = [
        pl.BlockSpec(memory_space=pl.ANY),
        pl.BlockSpec(memory_space=pl.ANY),
        pl.BlockSpec((tc, TOP_K), lambda i: (i, 0)),
        pl.BlockSpec((tc, d), lambda i: (i, 0)),
        pl.BlockSpec((None, 1, d), lambda i: (i // per_batch, 0, 0)),
    ]
    args = [pos_flat, ys, gates, x2, gf]
    if final_norm:
        in_specs.append(pl.BlockSpec((1, d), lambda i: (0, 0)))
        args.append(final_gain.reshape(1, d))
    return pl.pallas_call(
        functools.partial(_combine_kernel, final_norm=final_norm),
        out_shape=jax.ShapeDtypeStruct((n, d), F32),
        grid=(n // tc,),
        in_specs=in_specs,
        out_specs=pl.BlockSpec((tc, d), lambda i: (i, 0)),
        scratch_shapes=[
            pltpu.VMEM((TOP_K, tc * SUBLANES, LANES), F32),
            pltpu.SMEM((TOP_K * tc,), I32),
            pltpu.SemaphoreType.DMA,
            pltpu.SemaphoreType.DMA,
        ],
        compiler_params=_params("arbitrary"),
        name="moe_combine",
    )(*args)


def _split_w_in(w_in):
    d = w_in.shape[0]
    sizes = (GLA_QK, GLA_QK, GLA_V, GLA_GATE_RANK, GLA_V, GDN_QKV, GDN_HEADS, GDN_HEADS, GDN_V,
             SSD_INNER, SSD_XBC, SSD_HEADS, N_BRANCH * d)
    parts, acc = [], 0
    for s in sizes:
        parts.append(w_in[:, acc:acc + s])
        acc += s
    (gq, gk, gv, glr, gr, dqkv, da, db, dg, sz, sxbc, sdt, mrg) = parts
    zeros = lambda w: jnp.zeros((d, w), w_in.dtype)
    w_gla = jnp.concatenate([gq, gk, gv, gr, glr, zeros(LANES - GLA_GATE_RANK)], axis=1)
    w_gdn = jnp.concatenate([dqkv, dg, da, db, zeros(LANES - 2 * GDN_HEADS)], axis=1)
    w_ssd = jnp.concatenate([sz, sxbc, sdt, zeros(LANES - SSD_HEADS)], axis=1)
    pad_rows = lambda m: jnp.pad(m, ((0, GATE_ROWS - m.shape[0]), (0, 0)))
    w_gdn_t = pad_rows(jnp.concatenate([da, db], axis=1).T)
    w_ssd_t = pad_rows(sdt.T)
    cast = lambda m: m.astype(BF16)
    return cast(w_gla), cast(w_gdn), cast(w_gdn_t), cast(w_ssd), cast(w_ssd_t), cast(mrg)


def _moe(x2, scale, shift, gain, gf, final_gain, layer, w_router, b_router, w_gate_up, b_gate_up,
         w_down, b_down, *, seq, tm):
    n, d = x2.shape
    h3, idx_t, gate_t, rank_t, counts = _router(x2, scale, shift, gain, w_router.T, b_router,
                                                seq=seq, tm=tm)
    counts = counts[:, 0]
    padded = ((counts + MOE_BLOCK - 1) // MOE_BLOCK) * MOE_BLOCK
    pad_end = jnp.cumsum(padded)
    pad_start = pad_end - padded
    n_blocks = -(-(n * TOP_K) // MOE_BLOCK) + N_EXPERTS
    meta = jnp.concatenate([padded, pad_end, pad_end[-1:] // MOE_BLOCK]).astype(I32)
    block_start = jnp.arange(n_blocks, dtype=I32) * MOE_BLOCK
    block_e = jnp.minimum(jnp.sum((pad_end[None, :] <= block_start[:, None]).astype(I32), axis=1),
                          N_EXPERTS - 1)
    experts = jnp.arange(N_EXPERTS, dtype=I32)
    start_of = jnp.sum(jnp.where(idx_t[:TOP_K, :, None] == experts, pad_start, 0), axis=-1)
    pos = (start_of + rank_t[:TOP_K]).astype(I32)
    pos_flat = pos.T.reshape(-1)
    gates = gate_t[:TOP_K].T
    f = w_down.shape[2]
    wgu = _pair_regroup(w_gate_up, layer, tr=min(512, d))
    bgu = b_gate_up.reshape(N_EXPERTS, 2 * f // PAIR_TILE, LANES, 2).transpose(0, 1, 3, 2)
    xs = _dispatch(meta, pos_flat, h3, n_blocks * MOE_BLOCK, td=min(DISPATCH_TILE, n))
    ys = _experts(block_e, xs, wgu, bgu.reshape(N_EXPERTS, 1, 2 * f), w_down,
                  b_down.reshape(N_EXPERTS, 1, d), layer)
    return _combine(pos_flat, ys, gates, x2, gf, final_gain, seq=seq, tc=MOE_BLOCK)


def kernel(x, c, w_mod, b_mod, norm_mix, norm_ffn, norm_final, w_in, gla_w_gate2, gla_b_gate2, gla_norm, gdn_conv_w, gdn_a_log, gdn_dt_bias, gdn_norm, ssd_conv_w, ssd_conv_b, ssd_a_log, ssd_dt_bias, ssd_d, ssd_norm, w_branch_gla, w_branch_gdn, w_branch_ssd, b_merge, w_out, w_router, b_router, w_gate_up, b_gate_up, w_down, b_down):
    bsz, seq, d = x.shape
    depth = w_mod.shape[0]
    n = bsz * seq
    tm = min(512, seq)
    mod = _modulation(c, w_mod, b_mod)
    x2 = x.reshape(n, d)
    for l in range(depth):
        sh_m, sc_m, g_m, sh_f, sc_f, g_f = [mod[l, :, i * d:(i + 1) * d].reshape(bsz, 1, d)
                                            for i in range(6)]
        w_gla, w_gdn, w_gdn_t, w_ssd, w_ssd_t, w_mrg = _split_w_in(w_in[l])
        gain = norm_mix[l].reshape(1, d)
        proj = functools.partial(_inproj, x2, sc_m, sh_m, gain, seq=seq, tm=tm)
        p_gla = proj(w_gla, None)
        p_gdn, abt = proj(w_gdn, w_gdn_t)
        p_ssd, dtt = proj(w_ssd, w_ssd_t)
        p_mrg = proj(w_mrg, None)
        w2 = jnp.pad(gla_w_gate2[l], ((0, LANES - GLA_GATE_RANK), (0, 0))).astype(BF16)
        tile = min(MIXER_TILE, seq)
        y_gla = _gla(p_gla, w2, gla_b_gate2[l].reshape(1, GLA_QK), gla_norm[l].reshape(1, GLA_DV),
                     bsz=bsz, seq=seq, tile=tile)
        y_gdn = _gdn(p_gdn, abt, gdn_conv_w[l], gdn_a_log[l], gdn_dt_bias[l], gdn_norm[l],
                     bsz=bsz, seq=seq, tile=tile)
        y_ssd = _ssd(p_ssd, dtt, ssd_conv_w[l], ssd_conv_b[l], ssd_a_log[l],
                     ssd_dt_bias[l], ssd_d[l], ssd_norm[l], bsz=bsz, seq=seq, tile=tile)
        x2 = _merge(x2, y_gla, y_gdn, y_ssd, p_mrg, b_merge[l],
                    w_branch_gla[l].astype(BF16), w_branch_gdn[l].astype(BF16),
                    w_branch_ssd[l].astype(BF16), w_out[l].astype(BF16), g_m, seq=seq, tm=tm)
        final_gain = norm_final if l == depth - 1 else None
        x2 = _moe(x2, sc_f, sh_f, norm_ffn[l].reshape(1, d), g_f, final_gain, l, w_router[l],
                  b_router[l], w_gate_up, b_gate_up[l], w_down, b_down[l], seq=seq, tm=tm)
    return x2.reshape(bsz, seq, d)
```

```python
import functools

import jax
import jax.numpy as jnp
from jax import lax
from jax.experimental import pallas as pl
from jax.experimental.pallas import tpu as pltpu

F32 = jnp.float32
BF16 = jnp.bfloat16
I32 = jnp.int32
HIGHEST = lax.Precision.HIGHEST

CHUNK = 64
NORM_EPS = 1e-6
GLA_HEADS, GLA_DK, GLA_DV, GLA_GATE_RANK, GLA_GATE_TAU = 4, 64, 128, 16, 16.0
GDN_HEADS, GDN_DK, GDN_DV, GDN_CONV = 4, 128, 128, 4
SSD_HEADS, SSD_HEAD_DIM, SSD_GROUPS, SSD_STATE, SSD_CONV = 8, 64, 2, 64, 4
SSD_INNER = SSD_HEADS * SSD_HEAD_DIM
N_BRANCH = 3
N_EXPERTS, TOP_K = 32, 4
SWIGLU_LIMIT, SWIGLU_ALPHA = 7.0, 1.702
MOE_BLOCK = 512
COMBINE_TILE = 256

LANES = 128
SUBLANES = 8
DMA_GROUP = 8
DISPATCH_TILE = 1024
CONV_PAD = 8
GATE_ROWS = 8
MIXER_TILE = 4 * CHUNK
VMEM_LIMIT = 48 * 1024 * 1024


def _params(*sem):
    return pltpu.CompilerParams(dimension_semantics=sem, vmem_limit_bytes=VMEM_LIMIT)


def _mm(a, b):
    return lax.dot_general(a.astype(BF16), b.astype(BF16), (((1,), (0,)), ((), ())),
                           preferred_element_type=F32)


def _mm_nt(a, b):
    return lax.dot_general(a.astype(BF16), b.astype(BF16), (((1,), (1,)), ((), ())),
                           preferred_element_type=F32)


def _mm_tn(a, b):
    return lax.dot_general(a.astype(BF16), b.astype(BF16), (((0,), (0,)), ((), ())),
                           preferred_element_type=F32)


def _mm_f32(a, b):
    return lax.dot_general(a, b, (((1,), (0,)), ((), ())), precision=HIGHEST,
                           preferred_element_type=F32)


def _mm_nt_f32(a, b):
    return lax.dot_general(a, b, (((1,), (1,)), ((), ())), precision=HIGHEST,
                           preferred_element_type=F32)


def _split2(x):
    hi = x.astype(BF16)
    return hi, (x - hi.astype(F32)).astype(BF16)


def _split3(x):
    hi = x.astype(BF16)
    r = x - hi.astype(F32)
    mid = r.astype(BF16)
    return hi, mid, (r - mid.astype(F32)).astype(BF16)


def _mm_split(a, b):
    (a_hi, a_lo), (b_hi, b_lo) = a, b
    m = a_hi.shape[0]
    both = _mm(jnp.concatenate([a_hi, a_lo], axis=0), b_hi)
    return both[:m] + both[m:] + _mm(a_hi, b_lo)


def _sigmoid(x):
    return 1.0 / (1.0 + jnp.exp(-x))


def _silu(x):
    return x * _sigmoid(x)


def _softplus(x):
    return jnp.maximum(x, 0.0) + jnp.log1p(jnp.exp(-jnp.abs(x)))


def _log_sigmoid(x):
    return jnp.minimum(x, 0.0) - jnp.log1p(jnp.exp(-jnp.abs(x)))


def _rms(x):
    return x * lax.rsqrt(jnp.mean(x * x, axis=-1, keepdims=True) + NORM_EPS)


def _norm_mod(x, gain, scale, shift):
    return _rms(x) * gain * (1.0 + scale) + shift


def _chunk_masks():
    row = lax.broadcasted_iota(I32, (CHUNK, CHUNK), 0)
    col = lax.broadcasted_iota(I32, (CHUNK, CHUNK), 1)
    return row >= col, row > col


def _masked_decay(cum_col, cum_row, mask):
    return jnp.where(mask, jnp.exp(jnp.where(mask, cum_col - cum_row, 0.0)), 0.0)


def _mod_kernel(c_ref, w_ref, b_ref, o_ref):
    o_ref[...] = _mm_f32(_silu(c_ref[...]), w_ref[...]) + b_ref[...]


def _modulation(c, w_mod, b_mod):
    depth, d, six_d = w_mod.shape
    bsz = c.shape[0]
    n_col = six_d // d
    return pl.pallas_call(
        _mod_kernel,
        out_shape=jax.ShapeDtypeStruct((depth, bsz, six_d), F32),
        grid=(depth, n_col),
        in_specs=[
            pl.BlockSpec((bsz, d), lambda l, j: (0, 0)),
            pl.BlockSpec((None, d, d), lambda l, j: (l, 0, j)),
            pl.BlockSpec((None, 1, d), lambda l, j: (l, 0, j)),
        ],
        out_specs=pl.BlockSpec((None, bsz, d), lambda l, j: (l, 0, j)),
        compiler_params=_params("arbitrary", "arbitrary"),
        name="adaln_mod",
    )(c, w_mod, b_mod.reshape(depth, 1, six_d))


def _inproj_kernel(x_ref, sc_ref, sh_ref, g_ref, w_ref, *rest, has_t):
    h = _norm_mod(x_ref[...], g_ref[...], sc_ref[...], sh_ref[...]).astype(BF16)
    if has_t:
        wt_ref, o_ref, ot_ref = rest
        ot_ref[...] = _mm_nt(wt_ref[...], h)
    else:
        (o_ref,) = rest
    o_ref[...] = _mm(h, w_ref[...])


def _inproj(x2, scale, shift, gain, w, wt, *, seq, tm):
    n, d = x2.shape
    width = w.shape[1]
    per_batch = seq // tm
    batch_spec = pl.BlockSpec((None, 1, d), lambda i: (i // per_batch, 0, 0))
    in_specs = [
        pl.BlockSpec((tm, d), lambda i: (i, 0)),
        batch_spec, batch_spec,
        pl.BlockSpec((1, d), lambda i: (0, 0)),
        pl.BlockSpec((d, width), lambda i: (0, 0)),
    ]
    args = [x2, scale, shift, gain, w]
    out_shape = [jax.ShapeDtypeStruct((n, width), F32)]
    out_specs = [pl.BlockSpec((tm, width), lambda i: (i, 0))]
    if wt is not None:
        in_specs.append(pl.BlockSpec((GATE_ROWS, d), lambda i: (0, 0)))
        args.append(wt)
        out_shape.append(jax.ShapeDtypeStruct((GATE_ROWS, n), F32))
        out_specs.append(pl.BlockSpec((GATE_ROWS, tm), lambda i: (0, i)))
    outs = pl.pallas_call(
        functools.partial(_inproj_kernel, has_t=wt is not None),
        out_shape=out_shape, grid=(n // tm,), in_specs=in_specs, out_specs=out_specs,
        compiler_params=_params("arbitrary"),
        name="inproj",
    )(*args)
    return outs if wt is not None else outs[0]


GLA_QK = GLA_HEADS * GLA_DK
GLA_V = GLA_HEADS * GLA_DV
GLA_WIDTH = 2 * GLA_QK + 2 * GLA_V + LANES


def _chunk_cumsum_matrices(tile):
    r = jnp.arange(tile)
    tri = ((r[:, None] // CHUNK == r[None, :] // CHUNK) & (r[:, None] >= r[None, :])).astype(BF16)
    tri3 = jnp.concatenate([tri, tri, tri], axis=1)
    return tri3, tri3.T


def _gla_kernel(p_ref, tri_ref, w2_ref, b2_ref, ng_ref, o_ref, st_ref, oin_ref, qd_ref):
    tile = p_ref.shape[0]
    heads = range(GLA_HEADS)

    @pl.when(pl.program_id(1) == 0)
    def _():
        st_ref[...] = jnp.zeros_like(st_ref)

    incl, _ = _chunk_masks()
    lr = p_ref[:, 2 * GLA_QK + 2 * GLA_V:GLA_WIDTH]
    gk = _log_sigmoid(_mm(lr, w2_ref[...]) + b2_ref[...]) / GLA_GATE_TAU
    bcum = _mm(tri_ref[...], jnp.concatenate(_split3(gk), axis=0))
    ks = [slice(h * GLA_DK, (h + 1) * GLA_DK) for h in heads]
    vs = [slice(h * GLA_DV, (h + 1) * GLA_DV) for h in heads]

    upds, decays = [], []
    for c in range(tile // CHUNK):
        rows = slice(c * CHUNK, (c + 1) * CHUNK)
        b = bcum[rows]
        b_last = b[CHUNK - 1:CHUNK, :]
        q_dec = (p_ref[rows, 0:GLA_QK] * (GLA_DK ** -0.5) * jnp.exp(b)).astype(BF16)
        k = p_ref[rows, GLA_QK:2 * GLA_QK]
        k_neg = (k * jnp.exp(-b)).astype(BF16)
        k_pos = (k * jnp.exp(b_last - b)).astype(BF16)
        v = [p_ref[rows, 2 * GLA_QK + h * GLA_DV:2 * GLA_QK + (h + 1) * GLA_DV].astype(BF16)
             for h in heads]
        atts = [jnp.where(incl, _mm_nt(q_dec[:, ks[h]], k_neg[:, ks[h]]), 0.0) for h in heads]
        for h in heads:
            oin_ref[rows, vs[h]] = _mm(atts[h], v[h])
        upds.append([_mm_tn(v[h], k_pos[:, ks[h]]) for h in heads])
        decays.append(jnp.exp(b_last))
        qd_ref[rows, :] = q_dec

    for c in range(tile // CHUNK):
        rows = slice(c * CHUNK, (c + 1) * CHUNK)
        sts = [st_ref[h] for h in heads]
        os = [oin_ref[rows, vs[h]] + _mm_nt(qd_ref[rows, ks[h]], sts[h]) for h in heads]
        for h in heads:
            st_ref[h] = sts[h] * decays[c][:, ks[h]] + upds[c][h]
        for h in heads:
            rs = slice(2 * GLA_QK + GLA_V + h * GLA_DV, 2 * GLA_QK + GLA_V + (h + 1) * GLA_DV)
            o = _rms(os[h]) * ng_ref[...]
            o_ref[rows, vs[h]] = (o * _silu(p_ref[rows, rs])).astype(o_ref.dtype)


def _gla(p, w2, b2, ng, *, bsz, seq, tile):
    nt = seq // tile
    tri3, _ = _chunk_cumsum_matrices(tile)
    const = lambda shape: pl.BlockSpec(shape, lambda b, c: (0,) * len(shape))
    return pl.pallas_call(
        _gla_kernel,
        out_shape=jax.ShapeDtypeStruct((bsz * seq, GLA_V), BF16),
        grid=(bsz, nt),
        in_specs=[
            pl.BlockSpec((tile, GLA_WIDTH), lambda b, c: (b * nt + c, 0)),
            const(tri3.shape), const((LANES, GLA_QK)), const((1, GLA_QK)), const((1, GLA_DV)),
        ],
        out_specs=pl.BlockSpec((tile, GLA_V), lambda b, c: (b * nt + c, 0)),
        scratch_shapes=[
            pltpu.VMEM((GLA_HEADS, GLA_DV, GLA_DK), F32),
            pltpu.VMEM((tile, GLA_V), F32),
            pltpu.VMEM((tile, GLA_QK), BF16),
        ],
        compiler_params=_params("arbitrary", "arbitrary"),
        name="gla_mixer",
    )(p, tri3, w2, b2, ng)


def _conv_window(ext_ref, w_ref, r0, taps):
    out = None
    for j in range(taps):
        start = CONV_PAD + r0 - (taps - 1) + j
        term = ext_ref[start:start + CHUNK, :] * w_ref[j:j + 1, :]
        out = term if out is None else out + term
    return out


SSD_BC = SSD_GROUPS * SSD_STATE
SSD_XBC = SSD_INNER + 2 * SSD_BC
SSD_WIDTH = SSD_INNER + SSD_XBC + LANES
SSD_HG = SSD_HEADS // SSD_GROUPS


def _ssd_kernel(p_ref, dtt_ref, tri_ref, triu_ref, expand_ref, cw_ref, cb_ref, alog_e_ref, dtb_ref,
                alog_c_ref, dtb_c_ref, dskip_ref, ng_ref, o_ref, ext_ref, st_ref, y_ref, xs_ref, cm_ref):
    tile = p_ref.shape[0]
    heads = range(SSD_HEADS)
    groups = range(SSD_GROUPS)
    gw = SSD_INNER // SSD_GROUPS

    @pl.when(pl.program_id(1) == 0)
    def _():
        st_ref[...] = jnp.zeros_like(st_ref)
        ext_ref[0:CONV_PAD, :] = jnp.zeros((CONV_PAD, SSD_XBC), F32)

    incl, _ = _chunk_masks()
    ext_ref[CONV_PAD:CONV_PAD + tile, :] = p_ref[:, SSD_INNER:SSD_INNER + SSD_XBC]
    dt_col = _softplus(p_ref[:, SSD_INNER + SSD_XBC:SSD_WIDTH] + dtb_ref[...])
    dt_e = _mm(jnp.concatenate(_split3(dt_col), axis=1), expand_ref[...])
    cum_e = _mm(tri_ref[...], jnp.concatenate(_split3(dt_e * (-jnp.exp(alog_e_ref[...]))), axis=0))
    dt_row = _softplus(dtt_ref[...] + dtb_c_ref[...])
    cum_row = _mm(jnp.concatenate(_split3(dt_row * (-jnp.exp(alog_c_ref[...]))), axis=1),
                  triu_ref[...])
    hs = [slice(h * SSD_HEAD_DIM, (h + 1) * SSD_HEAD_DIM) for h in heads]
    gs = [slice(g * gw, (g + 1) * gw) for g in groups]

    upds, st_decays = [], []
    for c in range(tile // CHUNK):
        r0 = c * CHUNK
        rows = slice(r0, r0 + CHUNK)
        xbc = _silu(_conv_window(ext_ref, cw_ref, r0, SSD_CONV) + cb_ref[...])
        xs = xbc[:, 0:SSD_INNER]
        xs_ref[rows, :] = xs
        cm_ref[rows, :] = xbc[:, SSD_INNER + SSD_BC:SSD_XBC].astype(BF16)
        bm = [xbc[:, SSD_INNER + g * SSD_STATE:SSD_INNER + (g + 1) * SSD_STATE] for g in groups]
        cm = [xbc[:, SSD_INNER + SSD_BC + g * SSD_STATE:SSD_INNER + SSD_BC + (g + 1) * SSD_STATE]
              for g in groups]
        ce = cum_e[rows]
        cle = ce[CHUNK - 1:CHUNK, :]
        xdt = xs * dt_e[rows]
        xdt_bf = xdt.astype(BF16)
        xdec = (xdt * jnp.exp(cle - ce)).astype(BF16)
        cb = [_mm_nt(cm[g], bm[g]) for g in groups]
        segs = [_masked_decay(ce[:, hs[h]], cum_row[h:h + 1, rows], incl) for h in heads]
        for h in heads:
            y_ref[rows, hs[h]] = _mm(cb[h // SSD_HG] * segs[h], xdt_bf[:, hs[h]])
        upds.append([_mm_tn(bm[g], xdec[:, gs[g]]) for g in groups])
        st_decays.append(jnp.exp(cle))
    ext_ref[0:CONV_PAD, :] = ext_ref[tile:tile + CONV_PAD, :]

    for c in range(tile // CHUNK):
        rows = slice(c * CHUNK, (c + 1) * CHUNK)
        sts = [st_ref[g] for g in groups]
        y_in = [_mm(cm_ref[rows, g * SSD_STATE:(g + 1) * SSD_STATE], sts[g]) for g in groups]
        for g in groups:
            st_ref[g] = st_decays[c][:, gs[g]] * sts[g] + upds[c][g]
        y = y_ref[rows, :] + jnp.concatenate(y_in, axis=1) * jnp.exp(cum_e[rows])
        y = (y + dskip_ref[...] * xs_ref[rows, :]) * _silu(p_ref[rows, 0:SSD_INNER])
        for g in groups:
            o_ref[rows, gs[g]] = (_rms(y[:, gs[g]]) * ng_ref[:, gs[g]]).astype(o_ref.dtype)


def _ssd(p, dtt, conv_w, conv_b, a_log, dt_bias, d_skip, ng, *, bsz, seq, tile):
    nt = seq // tile
    pad = lambda v: jnp.pad(v, (0, LANES - v.shape[0])).reshape(1, LANES)
    col = lambda v: v.reshape(SSD_HEADS, 1)
    const = lambda shape: pl.BlockSpec(shape, lambda b, c: (0,) * len(shape))
    tri3, triu3 = _chunk_cumsum_matrices(tile)
    per_channel = lambda v: jnp.repeat(v, SSD_HEAD_DIM).reshape(1, SSD_INNER)
    expand = (jnp.arange(LANES)[:, None] == jnp.arange(SSD_INNER)[None, :] // SSD_HEAD_DIM)
    expand3 = jnp.concatenate([expand.astype(BF16)] * 3, axis=0)
    return pl.pallas_call(
        _ssd_kernel,
        out_shape=jax.ShapeDtypeStruct((bsz * seq, SSD_INNER), BF16),
        grid=(bsz, nt),
        in_specs=[
            pl.BlockSpec((tile, SSD_WIDTH), lambda b, c: (b * nt + c, 0)),
            pl.BlockSpec((GATE_ROWS, tile), lambda b, c: (0, b * nt + c)),
            const(tri3.shape), const(triu3.shape), const(expand3.shape),
            const((SSD_CONV, SSD_XBC)), const((1, SSD_XBC)),
            const((1, SSD_INNER)), const((1, LANES)),
            const((SSD_HEADS, 1)), const((SSD_HEADS, 1)),
            const((1, SSD_INNER)), const((1, SSD_INNER)),
        ],
        out_specs=pl.BlockSpec((tile, SSD_INNER), lambda b, c: (b * nt + c, 0)),
        scratch_shapes=[
            pltpu.VMEM((CONV_PAD + tile, SSD_XBC), F32),
            pltpu.VMEM((SSD_GROUPS, SSD_STATE, SSD_INNER // SSD_GROUPS), F32),
            pltpu.VMEM((tile, SSD_INNER), F32),
            pltpu.VMEM((tile, SSD_INNER), F32),
            pltpu.VMEM((tile, SSD_BC), BF16),
        ],
        compiler_params=_params("arbitrary", "arbitrary"),
        name="ssd_mixer",
    )(p, dtt, tri3, triu3, expand3, conv_w, conv_b.reshape(1, SSD_XBC), per_channel(a_log),
      pad(dt_bias), col(a_log), col(dt_bias), per_channel(d_skip), ng.reshape(1, SSD_INNER))


GDN_QK = GDN_HEADS * GDN_DK
GDN_V = GDN_HEADS * GDN_DV
GDN_QKV = 2 * GDN_QK + GDN_V
GDN_WIDTH = GDN_QKV + GDN_V + LANES


def _l2norm(x):
    return x * lax.rsqrt(jnp.sum(x * x, axis=-1, keepdims=True) + 1e-6)


def _unit_lower_inverses(mats):
    shape = mats[0].shape
    eye = jnp.where(lax.broadcasted_iota(I32, shape, 0) == lax.broadcasted_iota(I32, shape, 1),
                    1.0, 0.0)
    ps = [eye - a for a in mats]
    n_splits = [_split2(-a) for a in mats]
    steps = (shape[0] - 1).bit_length() - 1
    for _ in range(steps):
        n_splits = [_split2(_mm_split(ns, ns)) for ns in n_splits]
        ps = [p + _mm_split(_split2(p), ns) for p, ns in zip(ps, n_splits)]
    return ps


def _gdn_kernel(p_ref, abt_ref, tri_ref, triu_ref, cw_ref, alog_ref, dtb_ref, alog_c_ref, dtb_c_ref,
                ng_ref, o_ref, ext_ref, st_ref, upre_ref, wmix_ref, qdec_ref, kdec_ref, pmat_ref):
    tile = p_ref.shape[0]

    @pl.when(pl.program_id(1) == 0)
    def _():
        st_ref[...] = jnp.zeros_like(st_ref)
        ext_ref[0:CONV_PAD, :] = jnp.zeros((CONV_PAD, GDN_QKV), F32)

    incl, strict = _chunk_masks()
    ext_ref[CONV_PAD:CONV_PAD + tile, :] = p_ref[:, 0:GDN_QKV]
    ab = p_ref[:, GDN_QKV + GDN_V:GDN_WIDTH]
    beta_all = _sigmoid(ab)
    g_col = -jnp.exp(alog_ref[...]) * _softplus(ab + dtb_ref[...])
    cum_col = _mm(tri_ref[...], jnp.concatenate(_split3(g_col), axis=0))
    g_row = -jnp.exp(alog_c_ref[...]) * _softplus(abt_ref[...] + dtb_c_ref[...])
    cum_row = _mm(jnp.concatenate(_split3(g_row), axis=1), triu_ref[...])

    a_mats, rhss, where = [], [], []
    for c in range(tile // CHUNK):
        r0 = c * CHUNK
        rows = slice(r0, r0 + CHUNK)
        qkv = _silu(_conv_window(ext_ref, cw_ref, r0, GDN_CONV))
        for h in range(GDN_HEADS):
            hs = slice(h * GDN_DV, (h + 1) * GDN_DV)
            q = _l2norm(qkv[:, h * GDN_DK:(h + 1) * GDN_DK]) * (GDN_DK ** -0.5)
            k = _l2norm(qkv[:, GDN_QK + h * GDN_DK:GDN_QK + (h + 1) * GDN_DK])
            v = qkv[:, 2 * GDN_QK + h * GDN_DV:2 * GDN_QK + (h + 1) * GDN_DV]
            beta = beta_all[rows, GDN_HEADS + h:GDN_HEADS + h + 1]
            cc = cum_col[rows, h:h + 1]
            cl = cum_col[r0 + CHUNK - 1:r0 + CHUNK, h:h + 1]
            gam = _masked_decay(cc, cum_row[h:h + 1, rows], incl)
            a_mats.append(jnp.where(strict, beta * _mm_nt(k, k) * gam, 0.0))
            rhss.append(_split2(jnp.concatenate([beta * v, (beta * jnp.exp(cc)) * k], axis=1)))
            where.append((rows, hs))
            qdec_ref[rows, hs] = (q * jnp.exp(cc)).astype(BF16)
            kdec_ref[rows, hs] = (k * jnp.exp(cl - cc)).astype(BF16)
            pmat_ref[rows, h * CHUNK:(h + 1) * CHUNK] = (_mm_nt(q, k) * gam).astype(BF16)
    ext_ref[0:CONV_PAD, :] = ext_ref[tile:tile + CONV_PAD, :]
    t_invs = _unit_lower_inverses(a_mats)
    for t_inv, rhs, (rows, hs) in zip(t_invs, rhss, where):
        sol = _mm_split(_split2(t_inv), rhs)
        upre_ref[rows, hs] = sol[:, 0:GDN_DV]
        wmix_ref[rows, hs] = sol[:, GDN_DV:2 * GDN_DV].astype(BF16)

    heads = range(GDN_HEADS)
    for c in range(tile // CHUNK):
        r0 = c * CHUNK
        rows = slice(r0, r0 + CHUNK)
        hss = [slice(h * GDN_DV, (h + 1) * GDN_DV) for h in heads]
        ms = [st_ref[h] for h in heads]
        us = [upre_ref[rows, hss[h]] - _mm(wmix_ref[rows, hss[h]], ms[h]) for h in heads]
        os = [_mm(qdec_ref[rows, hss[h]], ms[h])
              + _mm(pmat_ref[rows, h * CHUNK:(h + 1) * CHUNK], us[h]) for h in heads]
        for h in heads:
            cl = cum_col[r0 + CHUNK - 1:r0 + CHUNK, h:h + 1]
            st_ref[h] = jnp.exp(cl) * ms[h] + _mm_tn(kdec_ref[rows, hss[h]], us[h])
        for h in heads:
            o = _rms(os[h]) * ng_ref[...]
            gs = slice(GDN_QKV + h * GDN_DV, GDN_QKV + (h + 1) * GDN_DV)
            o_ref[rows, hss[h]] = (o * _silu(p_ref[rows, gs])).astype(o_ref.dtype)


def _gdn(p, abt, conv_w, a_log, dt_bias, ng, *, bsz, seq, tile):
    nt = seq // tile
    pad = lambda v: jnp.pad(v, (0, LANES - v.shape[0])).reshape(1, LANES)
    col = lambda v: jnp.pad(v, (0, GATE_ROWS - v.shape[0])).reshape(GATE_ROWS, 1)
    const = lambda shape: pl.BlockSpec(shape, lambda b, c: (0,) * len(shape))
    tri3, triu3 = _chunk_cumsum_matrices(tile)
    return pl.pallas_call(
        _gdn_kernel,
        out_shape=jax.ShapeDtypeStruct((bsz * seq, GDN_V), BF16),
        grid=(bsz, nt),
        in_specs=[
            pl.BlockSpec((tile, GDN_WIDTH), lambda b, c: (b * nt + c, 0)),
            pl.BlockSpec((GATE_ROWS, tile), lambda b, c: (0, b * nt + c)),
            const(tri3.shape), const(triu3.shape),
            const((GDN_CONV, GDN_QKV)),
            const((1, LANES)), const((1, LANES)),
            const((GATE_ROWS, 1)), const((GATE_ROWS, 1)),
            const((1, GDN_DV)),
        ],
        out_specs=pl.BlockSpec((tile, GDN_V), lambda b, c: (b * nt + c, 0)),
        scratch_shapes=[
            pltpu.VMEM((CONV_PAD + tile, GDN_QKV), F32),
            pltpu.VMEM((GDN_HEADS, GDN_DK, GDN_DV), F32),
            pltpu.VMEM((tile, GDN_V), F32),
            pltpu.VMEM((tile, GDN_V), BF16),
            pltpu.VMEM((tile, GDN_QK), BF16),
            pltpu.VMEM((tile, GDN_QK), BF16),
            pltpu.VMEM((tile, GDN_HEADS * CHUNK), BF16),
        ],
        compiler_params=_params("arbitrary", "arbitrary"),
        name="gdn_mixer",
    )(p, abt, tri3, triu3, conv_w, pad(a_log), pad(dt_bias), col(a_log), col(dt_bias),
      ng.reshape(1, GDN_DV))


def _merge_kernel(x_ref, ya_ref, yb_ref, yc_ref, mr_ref, bm_ref, wa_ref, wb_ref, wc_ref, wo_ref,
                  gm_ref, o_ref):
    d = x_ref.shape[1]
    merged = None
    for i, (y_ref, w_ref) in enumerate(((ya_ref, wa_ref), (yb_ref, wb_ref), (yc_ref, wc_ref))):
        gate = _sigmoid(mr_ref[:, i * d:(i + 1) * d] + bm_ref[:, i * d:(i + 1) * d])
        term = gate * _mm(y_ref[...], w_ref[...])
        merged = term if merged is None else merged + term
    o_ref[...] = x_ref[...] + gm_ref[...] * _mm(merged, wo_ref[...])


def _merge(x2, ya, yb, yc, mr, b_merge, wa, wb, wc, wo, gm, *, seq, tm):
    n, d = x2.shape
    per_batch = seq // tm
    row = lambda w: pl.BlockSpec((tm, w), lambda i: (i, 0))
    const = lambda shape: pl.BlockSpec(shape, lambda i: (0,) * len(shape))
    return pl.pallas_call(
        _merge_kernel,
        out_shape=jax.ShapeDtypeStruct((n, d), F32),
        grid=(n // tm,),
        in_specs=[
            row(d), row(ya.shape[1]), row(yb.shape[1]), row(yc.shape[1]), row(N_BRANCH * d),
            const((1, N_BRANCH * d)),
            const(wa.shape), const(wb.shape), const(wc.shape), const(wo.shape),
            pl.BlockSpec((None, 1, d), lambda i: (i // per_batch, 0, 0)),
        ],
        out_specs=row(d),
        compiler_params=_params("arbitrary"),
        name="merge_out",
    )(x2, ya, yb, yc, mr, b_merge.reshape(1, N_BRANCH * d), wa, wb, wc, wo, gm)


def _router_kernel(x_ref, sc_ref, sh_ref, g_ref, wr_ref, br_ref,
                   h_ref, idx_ref, gate_ref, rank_ref, cnt_ref, carry_ref):
    @pl.when(pl.program_id(0) == 0)
    def _():
        carry_ref[...] = jnp.zeros_like(carry_ref)

    tm = x_ref.shape[0]
    h = _norm_mod(x_ref[...], g_ref[...], sc_ref[...], sh_ref[...])
    _store_token_tiles(h_ref, h)
    logits = _mm_nt_f32(wr_ref[...], h) + br_ref[...]
    e_iota = lax.broadcasted_iota(I32, logits.shape, 0).astype(F32)
    vals, idxs, sels = [], [], []
    cur = logits
    for _ in range(TOP_K):
        m = jnp.max(cur, axis=0, keepdims=True)
        idx = jnp.min(jnp.where(cur == m, e_iota, float(N_EXPERTS)), axis=0, keepdims=True)
        sel = e_iota == idx
        cur = jnp.where(sel, -jnp.inf, cur)
        vals.append(m)
        idxs.append(idx.astype(I32))
        sels.append(sel)
    exps = [jnp.exp(v - vals[0]) for v in vals]
    denom = exps[0] + exps[1] + exps[2] + exps[3]
    onehot = jnp.zeros(logits.shape, F32)
    for sel in sels:
        onehot = onehot + jnp.where(sel, 1.0, 0.0)
    row = lax.broadcasted_iota(I32, (tm, tm), 0)
    col = lax.broadcasted_iota(I32, (tm, tm), 1)
    before = _mm(onehot, jnp.where(row < col, 1.0, 0.0)) + carry_ref[...]
    zeros_i = jnp.zeros((GATE_ROWS - TOP_K, tm), I32)
    idx_ref[...] = jnp.concatenate(idxs + [zeros_i], axis=0)
    gate_ref[...] = jnp.concatenate([e / denom for e in exps] + [zeros_i.astype(F32)], axis=0)
    ranks = [jnp.sum(jnp.where(sel, before, 0.0), axis=0, keepdims=True).astype(I32) for sel in sels]
    rank_ref[...] = jnp.concatenate(ranks + [zeros_i], axis=0)
    carry_ref[...] = carry_ref[...] + jnp.sum(onehot, axis=1, keepdims=True)
    cnt_ref[...] = jnp.broadcast_to(carry_ref[...], cnt_ref.shape).astype(I32)


def _router(x2, scale, shift, gain, w_router_t, b_router, *, seq, tm):
    n, d = x2.shape
    per_batch = seq // tm
    batch_spec = pl.BlockSpec((None, 1, d), lambda i: (i // per_batch, 0, 0))
    lane_out = pl.BlockSpec((GATE_ROWS, tm), lambda i: (0, i))
    return pl.pallas_call(
        _router_kernel,
        out_shape=[
            jax.ShapeDtypeStruct((n * SUBLANES, LANES), F32),
            jax.ShapeDtypeStruct((GATE_ROWS, n), I32),
            jax.ShapeDtypeStruct((GATE_ROWS, n), F32),
            jax.ShapeDtypeStruct((GATE_ROWS, n), I32),
            jax.ShapeDtypeStruct((N_EXPERTS, LANES), I32),
        ],
        grid=(n // tm,),
        in_specs=[
            pl.BlockSpec((tm, d), lambda i: (i, 0)),
            batch_spec, batch_spec,
            pl.BlockSpec((1, d), lambda i: (0, 0)),
            pl.BlockSpec((N_EXPERTS, d), lambda i: (0, 0)),
            pl.BlockSpec((N_EXPERTS, 1), lambda i: (0, 0)),
        ],
        out_specs=[
            pl.BlockSpec((tm * SUBLANES, LANES), lambda i: (i, 0)),
            lane_out, lane_out, lane_out,
            pl.BlockSpec((N_EXPERTS, LANES), lambda i: (0, 0)),
        ],
        scratch_shapes=[pltpu.VMEM((N_EXPERTS, 1), F32)],
        compiler_params=_params("arbitrary"),
        name="router",
    )(x2, scale, shift, gain, w_router_t, b_router.reshape(N_EXPERTS, 1))


def _store_token_tiles(ref, value):
    rows = value.shape[0]
    for s in range(SUBLANES):
        ref[pl.ds(s, rows, stride=SUBLANES), :] = value[:, s * LANES:(s + 1) * LANES]


def _load_token_tiles(ref, rows):
    return jnp.concatenate([ref[pl.ds(s, rows, stride=SUBLANES), :] for s in range(SUBLANES)],
                           axis=1)


def _tile_copy(src, src_row, dst, dst_row, sem):
    src_at = pl.multiple_of(src_row * SUBLANES, SUBLANES)
    dst_at = pl.multiple_of(dst_row * SUBLANES, SUBLANES)
    return pltpu.make_async_copy(src.at[pl.ds(src_at, SUBLANES), :],
                                 dst.at[pl.ds(dst_at, SUBLANES), :], sem)


def _load_positions(pos_hbm, pos_smem, sem, count):
    start = pl.multiple_of(pl.program_id(0) * count, count)
    cp = pltpu.make_async_copy(pos_hbm.at[pl.ds(start, count)], pos_smem, sem)
    cp.start()
    cp.wait()


def _dispatch_kernel(meta_ref, pos_hbm, h_ref, xs_hbm, pos_smem, zero_ref, pos_sem, row_sem):
    td = pos_smem.shape[0] // TOP_K
    n_blocks = xs_hbm.shape[0] // (MOE_BLOCK * SUBLANES)
    block_rows = MOE_BLOCK * SUBLANES

    @pl.when(pl.program_id(0) == 0)
    def _():
        zero_ref[...] = jnp.zeros_like(zero_ref)

        def fill(block):
            at = pl.multiple_of(block * block_rows, block_rows)
            return pltpu.make_async_copy(zero_ref, xs_hbm.at[pl.ds(at, block_rows), :], row_sem)

        def each_fill(act):
            for e in range(N_EXPERTS):
                @pl.when(meta_ref[e] > 0)
                def _():
                    act(fill(meta_ref[N_EXPERTS + e] // MOE_BLOCK - 1))

                @pl.when(meta_ref[2 * N_EXPERTS] + e < n_blocks)
                def _():
                    act(fill(meta_ref[2 * N_EXPERTS] + e))

        each_fill(lambda cp: cp.start())
        each_fill(lambda cp: cp.wait())

    _load_positions(pos_hbm, pos_smem, pos_sem, TOP_K * td)

    def start(g, carry):
        t0 = g * DMA_GROUP
        slots = [pos_smem[TOP_K * t0 + j] for j in range(TOP_K * DMA_GROUP)]
        for j, slot in enumerate(slots):
            _tile_copy(h_ref, t0 + j // TOP_K, xs_hbm, slot, row_sem).start(priority=j % 2)
        return carry

    def wait(g, carry):
        for _ in range(TOP_K * DMA_GROUP):
            _tile_copy(h_ref, 0, xs_hbm, 0, row_sem).wait()
        return carry

    lax.fori_loop(0, td // DMA_GROUP, start, 0)
    lax.fori_loop(0, td // DMA_GROUP, wait, 0)


def _dispatch(meta, pos_flat, h3, n_rows, *, td):
    n = h3.shape[0] // SUBLANES
    grid_spec = pltpu.PrefetchScalarGridSpec(
        num_scalar_prefetch=1,
        grid=(n // td,),
        in_specs=[pl.BlockSpec(memory_space=pl.ANY),
                  pl.BlockSpec((td * SUBLANES, LANES), lambda i, meta: (i, 0))],
        out_specs=pl.BlockSpec(memory_space=pl.ANY),
        scratch_shapes=[
            pltpu.SMEM((TOP_K * td,), I32),
            pltpu.VMEM((MOE_BLOCK * SUBLANES, LANES), F32),
            pltpu.SemaphoreType.DMA,
            pltpu.SemaphoreType.DMA,
        ],
    )
    return pl.pallas_call(
        _dispatch_kernel,
        out_shape=jax.ShapeDtypeStruct((n_rows * SUBLANES, LANES), F32),
        grid_spec=grid_spec,
        compiler_params=_params("arbitrary"),
        name="moe_dispatch",
    )(meta, pos_flat, h3)


PAIR_TILE = 2 * LANES
EXPERT_CHUNK = 256


def _pair_regroup_kernel(w_ref, o_ref):
    r = lax.broadcasted_iota(I32, (PAIR_TILE, PAIR_TILE), 0)
    c = lax.broadcasted_iota(I32, (PAIR_TILE, PAIR_TILE), 1)
    src = jnp.where(c < LANES, 2 * c, 2 * (c - LANES) + 1)
    perm = jnp.where(r == src, 1.0, 0.0).astype(BF16)
    for t in range(w_ref.shape[1] // PAIR_TILE):
        cols = slice(t * PAIR_TILE, (t + 1) * PAIR_TILE)
        o_ref[:, cols] = _mm(w_ref[:, cols], perm).astype(o_ref.dtype)


def _pair_regroup(w_all, layer, *, tr):
    _, e, d, width = w_all.shape
    return pl.pallas_call(
        _pair_regroup_kernel,
        out_shape=jax.ShapeDtypeStruct((e, d, width), BF16),
        grid=(e, d // tr),
        in_specs=[pl.BlockSpec((None, None, tr, width), lambda i, j: (layer, i, j, 0))],
        out_specs=pl.BlockSpec((None, tr, width), lambda i, j: (i, j, 0)),
        compiler_params=_params("arbitrary", "arbitrary"),
        name="gate_up_regroup",
    )(w_all)


def _expert_kernel(be_ref, x_ref, wgu_ref, bgu_ref, wd_ref, bd_ref, o_ref):
    f = wd_ref.shape[0]
    n_chunks = f // EXPERT_CHUNK
    gu_cols = 2 * EXPERT_CHUNK
    used = pl.program_id(0) < be_ref[pl.num_programs(0)]

    @pl.when(used)
    def _():
        x = _load_token_tiles(x_ref, MOE_BLOCK).astype(BF16)

        def gate_up(c):
            cols = slice(c * gu_cols, (c + 1) * gu_cols)
            return _mm(x, wgu_ref[:, cols]) + bgu_ref[:, cols]

        y = bd_ref[...]
        gu = gate_up(0)
        for c in range(n_chunks):
            gu_next = gate_up(c + 1) if c + 1 < n_chunks else None
            acts = []
            for t in range(EXPERT_CHUNK // LANES):
                gate = jnp.minimum(gu[:, t * PAIR_TILE:t * PAIR_TILE + LANES], SWIGLU_LIMIT)
                up = jnp.clip(gu[:, t * PAIR_TILE + LANES:(t + 1) * PAIR_TILE], -SWIGLU_LIMIT,
                              SWIGLU_LIMIT)
                acts.append(((up + 1.0) * gate * _sigmoid(SWIGLU_ALPHA * gate)).astype(BF16))
            y = y + _mm(jnp.concatenate(acts, axis=1),
                        wd_ref[c * EXPERT_CHUNK:(c + 1) * EXPERT_CHUNK, :])
            gu = gu_next
        _store_token_tiles(o_ref, y)

    @pl.when(jnp.logical_not(used))
    def _():
        o_ref[...] = jnp.zeros_like(o_ref)


def _experts(block_e, xs, wgu, bgu, wd_all, bd, layer):
    f, d = wd_all.shape[2:]
    block_rows = MOE_BLOCK * SUBLANES
    n_blocks = xs.shape[0] // block_rows
    grid_spec = pltpu.PrefetchScalarGridSpec(
        num_scalar_prefetch=1,
        grid=(n_blocks,),
        in_specs=[
            pl.BlockSpec((block_rows, LANES), lambda i, be: (i, 0)),
            pl.BlockSpec((None, d, 2 * f), lambda i, be: (be[i], 0, 0)),
            pl.BlockSpec((None, 1, 2 * f), lambda i, be: (be[i], 0, 0)),
            pl.BlockSpec((None, None, f, d), lambda i, be: (layer, be[i], 0, 0)),
            pl.BlockSpec((None, 1, d), lambda i, be: (be[i], 0, 0)),
        ],
        out_specs=pl.BlockSpec((block_rows, LANES), lambda i, be: (i, 0)),
    )
    return pl.pallas_call(
        _expert_kernel,
        out_shape=jax.ShapeDtypeStruct(xs.shape, F32),
        grid_spec=grid_spec,
        compiler_params=_params("arbitrary"),
        name="moe_experts",
    )(block_e, xs, wgu, bgu, wd_all, bd)


def _combine_kernel(pos_hbm, ys_hbm, gate_ref, x_ref, gf_ref, *rest, final_norm):
    if final_norm:
        ng_ref, o_ref, buf_ref, pos_smem, pos_sem, row_sem = rest
    else:
        o_ref, buf_ref, pos_smem, pos_sem, row_sem = rest
    tc = x_ref.shape[0]
    _load_positions(pos_hbm, pos_smem, pos_sem, TOP_K * tc)

    def start(g, carry):
        t0 = g * DMA_GROUP
        slots = [pos_smem[TOP_K * t0 + j] for j in range(TOP_K * DMA_GROUP)]
        for j, slot in enumerate(slots):
            _tile_copy(ys_hbm, slot, buf_ref.at[j % TOP_K], t0 + j // TOP_K,
                       row_sem).start(priority=j % 2)
        return carry

    def wait(g, carry):
        for _ in range(TOP_K * DMA_GROUP):
            _tile_copy(ys_hbm, 0, buf_ref.at[0], 0, row_sem).wait()
        return carry

    lax.fori_loop(0, tc // DMA_GROUP, start, 0)
    lax.fori_loop(0, tc // DMA_GROUP, wait, 0)
    gates = [jnp.broadcast_to(gate_ref[:, k:k + 1], (tc, LANES)) for k in range(TOP_K)]
    ys = []
    for s in range(SUBLANES):
        y = None
        for k in range(TOP_K):
            term = gates[k] * buf_ref[k, pl.ds(s, tc, stride=SUBLANES), :]
            y = term if y is None else y + term
        ys.append(y)
    out = x_ref[...] + gf_ref[...] * jnp.concatenate(ys, axis=1)
    if final_norm:
        out = _rms(out) * ng_ref[...]
    o_ref[...] = out


def _combine(pos_flat, ys, gates, x2, gf, final_gain, *, seq, tc):
    n, d = x2.shape
    per_batch = seq // tc
    final_norm = final_gain is not None
    in_specs = [
        pl.BlockSpec(memory_space=pl.ANY),
        pl.BlockSpec(memory_space=pl.ANY),
        pl.BlockSpec((tc, TOP_K), lambda i: (i, 0)),
        pl.BlockSpec((tc, d), lambda i: (i, 0)),
        pl.BlockSpec((None, 1, d), lambda i: (i // per_batch, 0, 0)),
    ]
    args = [pos_flat, ys, gates, x2, gf]
    if final_norm:
        in_specs.append(pl.BlockSpec((1, d), lambda i: (0, 0)))
        args.append(final_gain.reshape(1, d))
    return pl.pallas_call(
        functools.partial(_combine_kernel, final_norm=final_norm),
        out_shape=jax.ShapeDtypeStruct((n, d), F32),
        grid=(n // tc,),
        in_specs=in_specs,
        out_specs=pl.BlockSpec((tc, d), lambda i: (i, 0)),
        scratch_shapes=[
            pltpu.VMEM((TOP_K, tc * SUBLANES, LANES), F32),
            pltpu.SMEM((TOP_K * tc,), I32),
            pltpu.SemaphoreType.DMA,
            pltpu.SemaphoreType.DMA,
        ],
        compiler_params=_params("arbitrary"),
        name="moe_combine",
    )(*args)


def _split_w_in(w_in):
    d = w_in.shape[0]
    sizes = (GLA_QK, GLA_QK, GLA_V, GLA_GATE_RANK, GLA_V, GDN_QKV, GDN_HEADS, GDN_HEADS, GDN_V,
             SSD_INNER, SSD_XBC, SSD_HEADS, N_BRANCH * d)
    parts, acc = [], 0
    for s in sizes:
        parts.append(w_in[:, acc:acc + s])
        acc += s
    (gq, gk, gv, glr, gr, dqkv, da, db, dg, sz, sxbc, sdt, mrg) = parts
    zeros = lambda w: jnp.zeros((d, w), w_in.dtype)
    w_gla = jnp.concatenate([gq, gk, gv, gr, glr, zeros(LANES - GLA_GATE_RANK)], axis=1)
    w_gdn = jnp.concatenate([dqkv, dg, da, db, zeros(LANES - 2 * GDN_HEADS)], axis=1)
    w_ssd = jnp.concatenate([sz, sxbc, sdt, zeros(LANES - SSD_HEADS)], axis=1)
    pad_rows = lambda m: jnp.pad(m, ((0, GATE_ROWS - m.shape[0]), (0, 0)))
    w_gdn_t = pad_rows(jnp.concatenate([da, db], axis=1).T)
    w_ssd_t = pad_rows(sdt.T)
    cast = lambda m: m.astype(BF16)
    return cast(w_gla), cast(w_gdn), cast(w_gdn_t), cast(w_ssd), cast(w_ssd_t), cast(mrg)


def _moe(x2, scale, shift, gain, gf, final_gain, layer, w_router, b_router, w_gate_up, b_gate_up,
         w_down, b_down, *, seq, tm):
    n, d = x2.shape
    h3, idx_t, gate_t, rank_t, counts = _router(x2, scale, shift, gain, w_router.T, b_router,
                                                seq=seq, tm=tm)
    counts = counts[:, 0]
    padded = ((counts + MOE_BLOCK - 1) // MOE_BLOCK) * MOE_BLOCK
    pad_end = jnp.cumsum(padded)
    pad_start = pad_end - padded
    n_blocks = -(-(n * TOP_K) // MOE_BLOCK) + N_EXPERTS
    n_used = pad_end[-1:] // MOE_BLOCK
    meta = jnp.concatenate([padded, pad_end, n_used]).astype(I32)
    block_start = jnp.arange(n_blocks, dtype=I32) * MOE_BLOCK
    block_e = jnp.minimum(jnp.sum((pad_end[None, :] <= block_start[:, None]).astype(I32), axis=1),
                          N_EXPERTS - 1)
    block_e = jnp.concatenate([block_e, n_used]).astype(I32)
    experts = jnp.arange(N_EXPERTS, dtype=I32)
    start_of = jnp.sum(jnp.where(idx_t[:TOP_K, :, None] == experts, pad_start, 0), axis=-1)
    pos = (start_of + rank_t[:TOP_K]).astype(I32)
    pos_flat = pos.T.reshape(-1)
    gates = gate_t[:TOP_K].T
    f = w_down.shape[2]
    wgu = _pair_regroup(w_gate_up, layer, tr=min(512, d))
    bgu = b_gate_up.reshape(N_EXPERTS, 2 * f // PAIR_TILE, LANES, 2).transpose(0, 1, 3, 2)
    xs = _dispatch(meta, pos_flat, h3, n_blocks * MOE_BLOCK, td=min(DISPATCH_TILE, n))
    ys = _experts(block_e, xs, wgu, bgu.reshape(N_EXPERTS, 1, 2 * f), w_down,
                  b_down.reshape(N_EXPERTS, 1, d), layer)
    return _combine(pos_flat, ys, gates, x2, gf, final_gain, seq=seq, tc=min(COMBINE_TILE, seq))


def kernel(x, c, w_mod, b_mod, norm_mix, norm_ffn, norm_final, w_in, gla_w_gate2, gla_b_gate2, gla_norm, gdn_conv_w, gdn_a_log, gdn_dt_bias, gdn_norm, ssd_conv_w, ssd_conv_b, ssd_a_log, ssd_dt_bias, ssd_d, ssd_norm, w_branch_gla, w_branch_gdn, w_branch_ssd, b_merge, w_out, w_router, b_router, w_gate_up, b_gate_up, w_down, b_down):
    bsz, seq, d = x.shape
    depth = w_mod.shape[0]
    n = bsz * seq
    tm = min(512, seq)
    mod = _modulation(c, w_mod, b_mod)
    x2 = x.reshape(n, d)
    for l in range(depth):
        sh_m, sc_m, g_m, sh_f, sc_f, g_f = [mod[l, :, i * d:(i + 1) * d].reshape(bsz, 1, d)
                                            for i in range(6)]
        w_gla, w_gdn, w_gdn_t, w_ssd, w_ssd_t, w_mrg = _split_w_in(w_in[l])
        gain = norm_mix[l].reshape(1, d)
        proj = functools.partial(_inproj, x2, sc_m, sh_m, gain, seq=seq, tm=tm)
        p_gla = proj(w_gla, None)
        p_gdn, abt = proj(w_gdn, w_gdn_t)
        p_ssd, dtt = proj(w_ssd, w_ssd_t)
        p_mrg = proj(w_mrg, None)
        w2 = jnp.pad(gla_w_gate2[l], ((0, LANES - GLA_GATE_RANK), (0, 0))).astype(BF16)
        tile = min(MIXER_TILE, seq)
        y_gla = _gla(p_gla, w2, gla_b_gate2[l].reshape(1, GLA_QK), gla_norm[l].reshape(1, GLA_DV),
                     bsz=bsz, seq=seq, tile=tile)
        y_gdn = _gdn(p_gdn, abt, gdn_conv_w[l], gdn_a_log[l], gdn_dt_bias[l], gdn_norm[l],
                     bsz=bsz, seq=seq, tile=tile)
        y_ssd = _ssd(p_ssd, dtt, ssd_conv_w[l], ssd_conv_b[l], ssd_a_log[l],
                     ssd_dt_bias[l], ssd_d[l], ssd_norm[l], bsz=bsz, seq=seq, tile=tile)
        x2 = _merge(x2, y_gla, y_gdn, y_ssd, p_mrg, b_merge[l],
                    w_branch_gla[l].astype(BF16), w_branch_gdn[l].astype(BF16),
                    w_branch_ssd[l].astype(BF16), w_out[l].astype(BF16), g_m, seq=seq, tm=tm)
        final_gain = norm_final if l == depth - 1 else None
        x2 = _moe(x2, sc_f, sh_f, norm_ffn[l].reshape(1, d), g_f, final_gain, l, w_router[l],
                  b_router[l], w_gate_up, b_gate_up[l], w_down, b_down[l], seq=seq, tm=tm)
    return x2.reshape(bsz, seq, d)
```

```python
import functools
import math

import jax
import jax.numpy as jnp
from jax import lax
from jax.experimental import pallas as pl
from jax.experimental.pallas import tpu as pltpu

F32 = jnp.float32
BF16 = jnp.bfloat16
I32 = jnp.int32
HIGHEST = lax.Precision.HIGHEST

CHUNK = 64
NORM_EPS = 1e-6
GLA_HEADS, GLA_DK, GLA_DV, GLA_GATE_RANK, GLA_GATE_TAU = 4, 64, 128, 16, 16.0
GDN_HEADS, GDN_DK, GDN_DV, GDN_CONV = 4, 128, 128, 4
SSD_HEADS, SSD_HEAD_DIM, SSD_GROUPS, SSD_STATE, SSD_CONV = 8, 64, 2, 64, 4
SSD_INNER = SSD_HEADS * SSD_HEAD_DIM
N_BRANCH = 3
N_EXPERTS, TOP_K = 32, 4
SWIGLU_LIMIT, SWIGLU_ALPHA = 7.0, 1.702
MOE_BLOCK = 512
COMBINE_TILE = 256
POS_STAGE = 8

LANES = 128
SUBLANES = 8
DMA_GROUP = 8
DISPATCH_TILE = 2048
CONV_PAD = 8
GATE_ROWS = 8
MIXER_TILE = 4 * CHUNK
VMEM_LIMIT = 48 * 1024 * 1024


def _params(*sem):
    return pltpu.CompilerParams(dimension_semantics=sem, vmem_limit_bytes=VMEM_LIMIT)


def _mm(a, b):
    return lax.dot_general(a.astype(BF16), b.astype(BF16), (((1,), (0,)), ((), ())),
                           preferred_element_type=F32)


def _mm_nt(a, b):
    return lax.dot_general(a.astype(BF16), b.astype(BF16), (((1,), (1,)), ((), ())),
                           preferred_element_type=F32)


def _mm_tn(a, b):
    return lax.dot_general(a.astype(BF16), b.astype(BF16), (((0,), (0,)), ((), ())),
                           preferred_element_type=F32)


def _mm_f32(a, b):
    return lax.dot_general(a, b, (((1,), (0,)), ((), ())), precision=HIGHEST,
                           preferred_element_type=F32)


def _mm_nt_f32(a, b):
    return lax.dot_general(a, b, (((1,), (1,)), ((), ())), precision=HIGHEST,
                           preferred_element_type=F32)


def _split2(x):
    hi = x.astype(BF16)
    return hi, (x - hi.astype(F32)).astype(BF16)


def _split3(x):
    hi = x.astype(BF16)
    r = x - hi.astype(F32)
    mid = r.astype(BF16)
    return hi, mid, (r - mid.astype(F32)).astype(BF16)


def _mm_split(a, b):
    (a_hi, a_lo), (b_hi, b_lo) = a, b
    m = a_hi.shape[0]
    both = _mm(jnp.concatenate([a_hi, a_lo], axis=0), b_hi)
    return both[:m] + both[m:] + _mm(a_hi, b_lo)


def _sigmoid(x):
    return 1.0 / (1.0 + jnp.exp(-x))


def _silu(x):
    return x * _sigmoid(x)


def _softplus(x):
    return jnp.maximum(x, 0.0) + jnp.log1p(jnp.exp(-jnp.abs(x)))


def _log_sigmoid(x):
    return jnp.minimum(x, 0.0) - jnp.log1p(jnp.exp(-jnp.abs(x)))


def _rms(x):
    return x * lax.rsqrt(jnp.mean(x * x, axis=-1, keepdims=True) + NORM_EPS)


def _norm_mod(x, gain, scale, shift):
    return _rms(x) * gain * (1.0 + scale) + shift


def _chunk_masks():
    row = lax.broadcasted_iota(I32, (CHUNK, CHUNK), 0)
    col = lax.broadcasted_iota(I32, (CHUNK, CHUNK), 1)
    return row >= col, row > col


def _masked_decay(cum_col, cum_row, mask):
    return jnp.where(mask, jnp.exp(jnp.where(mask, cum_col - cum_row, 0.0)), 0.0)


def _mod_kernel(c_ref, w_ref, b_ref, o_ref):
    o_ref[...] = _mm_f32(_silu(c_ref[...]), w_ref[...]) + b_ref[...]


def _modulation(c, w_mod, b_mod):
    depth, d, six_d = w_mod.shape
    bsz = c.shape[0]
    n_col = six_d // d
    return pl.pallas_call(
        _mod_kernel,
        out_shape=jax.ShapeDtypeStruct((depth, bsz, six_d), F32),
        grid=(depth, n_col),
        in_specs=[
            pl.BlockSpec((bsz, d), lambda l, j: (0, 0)),
            pl.BlockSpec((None, d, d), lambda l, j: (l, 0, j)),
            pl.BlockSpec((None, 1, d), lambda l, j: (l, 0, j)),
        ],
        out_specs=pl.BlockSpec((None, bsz, d), lambda l, j: (l, 0, j)),
        compiler_params=_params("arbitrary", "arbitrary"),
        name="adaln_mod",
    )(c, w_mod, b_mod.reshape(depth, 1, six_d))


def _inproj_kernel(x_ref, sc_ref, sh_ref, g_ref, w_ref, *rest, has_t):
    h = _norm_mod(x_ref[...], g_ref[...], sc_ref[...], sh_ref[...]).astype(BF16)
    if has_t:
        wt_ref, o_ref, ot_ref = rest
        ot_ref[...] = _mm_nt(wt_ref[...], h)
    else:
        (o_ref,) = rest
    o_ref[...] = _mm(h, w_ref[...])


def _inproj(x2, scale, shift, gain, w, wt, *, seq, tm):
    n, d = x2.shape
    width = w.shape[1]
    per_batch = seq // tm
    batch_spec = pl.BlockSpec((None, 1, d), lambda i: (i // per_batch, 0, 0))
    in_specs = [
        pl.BlockSpec((tm, d), lambda i: (i, 0)),
        batch_spec, batch_spec,
        pl.BlockSpec((1, d), lambda i: (0, 0)),
        pl.BlockSpec((d, width), lambda i: (0, 0)),
    ]
    args = [x2, scale, shift, gain, w]
    out_shape = [jax.ShapeDtypeStruct((n, width), F32)]
    out_specs = [pl.BlockSpec((tm, width), lambda i: (i, 0))]
    if wt is not None:
        in_specs.append(pl.BlockSpec((GATE_ROWS, d), lambda i: (0, 0)))
        args.append(wt)
        out_shape.append(jax.ShapeDtypeStruct((GATE_ROWS, n), F32))
        out_specs.append(pl.BlockSpec((GATE_ROWS, tm), lambda i: (0, i)))
    outs = pl.pallas_call(
        functools.partial(_inproj_kernel, has_t=wt is not None),
        out_shape=out_shape, grid=(n // tm,), in_specs=in_specs, out_specs=out_specs,
        compiler_params=_params("arbitrary"),
        name="inproj",
    )(*args)
    return outs if wt is not None else outs[0]


GLA_QK = GLA_HEADS * GLA_DK
GLA_V = GLA_HEADS * GLA_DV
GLA_WIDTH = 2 * GLA_QK + 2 * GLA_V + LANES


def _chunk_cumsum_matrices(tile):
    r = jnp.arange(tile)
    tri = ((r[:, None] // CHUNK == r[None, :] // CHUNK) & (r[:, None] >= r[None, :])).astype(BF16)
    tri3 = jnp.concatenate([tri, tri, tri], axis=1)
    return tri3, tri3.T


def _gla_kernel(p_ref, tri_ref, w2_ref, b2_ref, ng_ref, o_ref, st_ref, oin_ref, qd_ref):
    tile = p_ref.shape[0]
    heads = range(GLA_HEADS)

    @pl.when(pl.program_id(1) == 0)
    def _():
        st_ref[...] = jnp.zeros_like(st_ref)

    incl, _ = _chunk_masks()
    lr = p_ref[:, 2 * GLA_QK + 2 * GLA_V:GLA_WIDTH]
    gk = _log_sigmoid(_mm(lr, w2_ref[...]) + b2_ref[...]) / GLA_GATE_TAU
    bcum = _mm(tri_ref[...], jnp.concatenate(_split3(gk), axis=0))
    ks = [slice(h * GLA_DK, (h + 1) * GLA_DK) for h in heads]
    vs = [slice(h * GLA_DV, (h + 1) * GLA_DV) for h in heads]

    upds, decays = [], []
    for c in range(tile // CHUNK):
        rows = slice(c * CHUNK, (c + 1) * CHUNK)
        b = bcum[rows]
        b_last = b[CHUNK - 1:CHUNK, :]
        q_dec = (p_ref[rows, 0:GLA_QK] * (GLA_DK ** -0.5) * jnp.exp(b)).astype(BF16)
        k = p_ref[rows, GLA_QK:2 * GLA_QK]
        k_neg = (k * jnp.exp(-b)).astype(BF16)
        k_pos = (k * jnp.exp(b_last - b)).astype(BF16)
        v = [p_ref[rows, 2 * GLA_QK + h * GLA_DV:2 * GLA_QK + (h + 1) * GLA_DV].astype(BF16)
             for h in heads]
        atts = [jnp.where(incl, _mm_nt(q_dec[:, ks[h]], k_neg[:, ks[h]]), 0.0) for h in heads]
        for h in heads:
            oin_ref[rows, vs[h]] = _mm(atts[h], v[h])
        upds.append([_mm_tn(v[h], k_pos[:, ks[h]]) for h in heads])
        decays.append(jnp.exp(b_last))
        qd_ref[rows, :] = q_dec

    for c in range(tile // CHUNK):
        rows = slice(c * CHUNK, (c + 1) * CHUNK)
        sts = [st_ref[h] for h in heads]
        os = [oin_ref[rows, vs[h]] + _mm_nt(qd_ref[rows, ks[h]], sts[h]) for h in heads]
        for h in heads:
            st_ref[h] = sts[h] * decays[c][:, ks[h]] + upds[c][h]
        for h in heads:
            rs = slice(2 * GLA_QK + GLA_V + h * GLA_DV, 2 * GLA_QK + GLA_V + (h + 1) * GLA_DV)
            o = _rms(os[h]) * ng_ref[...]
            o_ref[rows, vs[h]] = (o * _silu(p_ref[rows, rs])).astype(o_ref.dtype)


def _gla(p, w2, b2, ng, *, bsz, seq, tile):
    nt = seq // tile
    tri3, _ = _chunk_cumsum_matrices(tile)
    const = lambda shape: pl.BlockSpec(shape, lambda b, c: (0,) * len(shape))
    return pl.pallas_call(
        _gla_kernel,
        out_shape=jax.ShapeDtypeStruct((bsz * seq, GLA_V), BF16),
        grid=(bsz, nt),
        in_specs=[
            pl.BlockSpec((tile, GLA_WIDTH), lambda b, c: (b * nt + c, 0)),
            const(tri3.shape), const((LANES, GLA_QK)), const((1, GLA_QK)), const((1, GLA_DV)),
        ],
        out_specs=pl.BlockSpec((tile, GLA_V), lambda b, c: (b * nt + c, 0)),
        scratch_shapes=[
            pltpu.VMEM((GLA_HEADS, GLA_DV, GLA_DK), F32),
            pltpu.VMEM((tile, GLA_V), F32),
            pltpu.VMEM((tile, GLA_QK), BF16),
        ],
        compiler_params=_params("arbitrary", "arbitrary"),
        name="gla_mixer",
    )(p, tri3, w2, b2, ng)


def _conv_window(ext_ref, w_ref, r0, taps):
    out = None
    for j in range(taps):
        start = CONV_PAD + r0 - (taps - 1) + j
        term = ext_ref[start:start + CHUNK, :] * w_ref[j:j + 1, :]
        out = term if out is None else out + term
    return out


SSD_BC = SSD_GROUPS * SSD_STATE
SSD_XBC = SSD_INNER + 2 * SSD_BC
SSD_WIDTH = SSD_INNER + SSD_XBC + LANES
SSD_HG = SSD_HEADS // SSD_GROUPS


def _ssd_kernel(p_ref, dtt_ref, tri_ref, triu_ref, expand_ref, cw_ref, cb_ref, alog_e_ref, dtb_ref,
                alog_c_ref, dtb_c_ref, dskip_ref, ng_ref, o_ref, ext_ref, st_ref, y_ref, xs_ref, cm_ref):
    tile = p_ref.shape[0]
    heads = range(SSD_HEADS)
    groups = range(SSD_GROUPS)
    gw = SSD_INNER // SSD_GROUPS

    @pl.when(pl.program_id(1) == 0)
    def _():
        st_ref[...] = jnp.zeros_like(st_ref)
        ext_ref[0:CONV_PAD, :] = jnp.zeros((CONV_PAD, SSD_XBC), F32)

    incl, _ = _chunk_masks()
    ext_ref[CONV_PAD:CONV_PAD + tile, :] = p_ref[:, SSD_INNER:SSD_INNER + SSD_XBC]
    dt_col = _softplus(p_ref[:, SSD_INNER + SSD_XBC:SSD_WIDTH] + dtb_ref[...])
    dt_e = _mm(jnp.concatenate(_split3(dt_col), axis=1), expand_ref[...])
    cum_e = _mm(tri_ref[...], jnp.concatenate(_split3(dt_e * (-jnp.exp(alog_e_ref[...]))), axis=0))
    dt_row = _softplus(dtt_ref[...] + dtb_c_ref[...])
    cum_row = _mm(jnp.concatenate(_split3(dt_row * (-jnp.exp(alog_c_ref[...]))), axis=1),
                  triu_ref[...])
    hs = [slice(h * SSD_HEAD_DIM, (h + 1) * SSD_HEAD_DIM) for h in heads]
    gs = [slice(g * gw, (g + 1) * gw) for g in groups]

    upds, st_decays = [], []
    for c in range(tile // CHUNK):
        r0 = c * CHUNK
        rows = slice(r0, r0 + CHUNK)
        xbc = _silu(_conv_window(ext_ref, cw_ref, r0, SSD_CONV) + cb_ref[...])
        xs = xbc[:, 0:SSD_INNER]
        xs_ref[rows, :] = xs
        cm_ref[rows, :] = xbc[:, SSD_INNER + SSD_BC:SSD_XBC].astype(BF16)
        bm = [xbc[:, SSD_INNER + g * SSD_STATE:SSD_INNER + (g + 1) * SSD_STATE] for g in groups]
        cm = [xbc[:, SSD_INNER + SSD_BC + g * SSD_STATE:SSD_INNER + SSD_BC + (g + 1) * SSD_STATE]
              for g in groups]
        ce = cum_e[rows]
        cle = ce[CHUNK - 1:CHUNK, :]
        xdt = xs * dt_e[rows]
        xdt_bf = xdt.astype(BF16)
        xdec = (xdt * jnp.exp(cle - ce)).astype(BF16)
        cb = [_mm_nt(cm[g], bm[g]) for g in groups]
        segs = [_masked_decay(ce[:, hs[h]], cum_row[h:h + 1, rows], incl) for h in heads]
        for h in heads:
            y_ref[rows, hs[h]] = _mm(cb[h // SSD_HG] * segs[h], xdt_bf[:, hs[h]])
        upds.append([_mm_tn(bm[g], xdec[:, gs[g]]) for g in groups])
        st_decays.append(jnp.exp(cle))
    ext_ref[0:CONV_PAD, :] = ext_ref[tile:tile + CONV_PAD, :]

    for c in range(tile // CHUNK):
        rows = slice(c * CHUNK, (c + 1) * CHUNK)
        sts = [st_ref[g] for g in groups]
        y_in = [_mm(cm_ref[rows, g * SSD_STATE:(g + 1) * SSD_STATE], sts[g]) for g in groups]
        for g in groups:
            st_ref[g] = st_decays[c][:, gs[g]] * sts[g] + upds[c][g]
        y = y_ref[rows, :] + jnp.concatenate(y_in, axis=1) * jnp.exp(cum_e[rows])
        y = (y + dskip_ref[...] * xs_ref[rows, :]) * _silu(p_ref[rows, 0:SSD_INNER])
        for g in groups:
            o_ref[rows, gs[g]] = (_rms(y[:, gs[g]]) * ng_ref[:, gs[g]]).astype(o_ref.dtype)


def _ssd(p, dtt, conv_w, conv_b, a_log, dt_bias, d_skip, ng, *, bsz, seq, tile):
    nt = seq // tile
    pad = lambda v: jnp.pad(v, (0, LANES - v.shape[0])).reshape(1, LANES)
    col = lambda v: v.reshape(SSD_HEADS, 1)
    const = lambda shape: pl.BlockSpec(shape, lambda b, c: (0,) * len(shape))
    tri3, triu3 = _chunk_cumsum_matrices(tile)
    per_channel = lambda v: jnp.repeat(v, SSD_HEAD_DIM).reshape(1, SSD_INNER)
    expand = (jnp.arange(LANES)[:, None] == jnp.arange(SSD_INNER)[None, :] // SSD_HEAD_DIM)
    expand3 = jnp.concatenate([expand.astype(BF16)] * 3, axis=0)
    return pl.pallas_call(
        _ssd_kernel,
        out_shape=jax.ShapeDtypeStruct((bsz * seq, SSD_INNER), BF16),
        grid=(bsz, nt),
        in_specs=[
            pl.BlockSpec((tile, SSD_WIDTH), lambda b, c: (b * nt + c, 0)),
            pl.BlockSpec((GATE_ROWS, tile), lambda b, c: (0, b * nt + c)),
            const(tri3.shape), const(triu3.shape), const(expand3.shape),
            const((SSD_CONV, SSD_XBC)), const((1, SSD_XBC)),
            const((1, SSD_INNER)), const((1, LANES)),
            const((SSD_HEADS, 1)), const((SSD_HEADS, 1)),
            const((1, SSD_INNER)), const((1, SSD_INNER)),
        ],
        out_specs=pl.BlockSpec((tile, SSD_INNER), lambda b, c: (b * nt + c, 0)),
        scratch_shapes=[
            pltpu.VMEM((CONV_PAD + tile, SSD_XBC), F32),
            pltpu.VMEM((SSD_GROUPS, SSD_STATE, SSD_INNER // SSD_GROUPS), F32),
            pltpu.VMEM((tile, SSD_INNER), F32),
            pltpu.VMEM((tile, SSD_INNER), F32),
            pltpu.VMEM((tile, SSD_BC), BF16),
        ],
        compiler_params=_params("arbitrary", "arbitrary"),
        name="ssd_mixer",
    )(p, dtt, tri3, triu3, expand3, conv_w, conv_b.reshape(1, SSD_XBC), per_channel(a_log),
      pad(dt_bias), col(a_log), col(dt_bias), per_channel(d_skip), ng.reshape(1, SSD_INNER))


GDN_QK = GDN_HEADS * GDN_DK
GDN_V = GDN_HEADS * GDN_DV
GDN_QKV = 2 * GDN_QK + GDN_V
GDN_WIDTH = GDN_QKV + GDN_V + LANES


def _l2norm(x):
    return x * lax.rsqrt(jnp.sum(x * x, axis=-1, keepdims=True) + 1e-6)


def _unit_lower_inverses(mats):
    shape = mats[0].shape
    eye = jnp.where(lax.broadcasted_iota(I32, shape, 0) == lax.broadcasted_iota(I32, shape, 1),
                    1.0, 0.0)
    ps = [eye - a for a in mats]
    n_splits = [_split2(-a) for a in mats]
    steps = (shape[0] - 1).bit_length() - 1
    for _ in range(steps):
        n_splits = [_split2(_mm_split(ns, ns)) for ns in n_splits]
        ps = [p + _mm_split(_split2(p), ns) for p, ns in zip(ps, n_splits)]
    return ps


def _gdn_kernel(p_ref, abt_ref, tri_ref, triu_ref, cw_ref, alog_ref, dtb_ref, alog_c_ref, dtb_c_ref,
                ng_ref, o_ref, ext_ref, st_ref, upre_ref, wmix_ref, qdec_ref, kdec_ref, pmat_ref):
    tile = p_ref.shape[0]

    @pl.when(pl.program_id(1) == 0)
    def _():
        st_ref[...] = jnp.zeros_like(st_ref)
        ext_ref[0:CONV_PAD, :] = jnp.zeros((CONV_PAD, GDN_QKV), F32)

    incl, strict = _chunk_masks()
    ext_ref[CONV_PAD:CONV_PAD + tile, :] = p_ref[:, 0:GDN_QKV]
    ab = p_ref[:, GDN_QKV + GDN_V:GDN_WIDTH]
    beta_all = _sigmoid(ab)
    g_col = -jnp.exp(alog_ref[...]) * _softplus(ab + dtb_ref[...])
    cum_col = _mm(tri_ref[...], jnp.concatenate(_split3(g_col), axis=0))
    g_row = -jnp.exp(alog_c_ref[...]) * _softplus(abt_ref[...] + dtb_c_ref[...])
    cum_row = _mm(jnp.concatenate(_split3(g_row), axis=1), triu_ref[...])

    a_mats, rhss, where = [], [], []
    for c in range(tile // CHUNK):
        r0 = c * CHUNK
        rows = slice(r0, r0 + CHUNK)
        qkv = _silu(_conv_window(ext_ref, cw_ref, r0, GDN_CONV))
        for h in range(GDN_HEADS):
            hs = slice(h * GDN_DV, (h + 1) * GDN_DV)
            q = _l2norm(qkv[:, h * GDN_DK:(h + 1) * GDN_DK]) * (GDN_DK ** -0.5)
            k = _l2norm(qkv[:, GDN_QK + h * GDN_DK:GDN_QK + (h + 1) * GDN_DK])
            v = qkv[:, 2 * GDN_QK + h * GDN_DV:2 * GDN_QK + (h + 1) * GDN_DV]
            beta = beta_all[rows, GDN_HEADS + h:GDN_HEADS + h + 1]
            cc = cum_col[rows, h:h + 1]
            cl = cum_col[r0 + CHUNK - 1:r0 + CHUNK, h:h + 1]
            gam = _masked_decay(cc, cum_row[h:h + 1, rows], incl)
            a_mats.append(jnp.where(strict, beta * _mm_nt(k, k) * gam, 0.0))
            rhss.append(_split2(jnp.concatenate([beta * v, (beta * jnp.exp(cc)) * k], axis=1)))
            where.append((rows, hs))
            qdec_ref[rows, hs] = (q * jnp.exp(cc)).astype(BF16)
            kdec_ref[rows, hs] = (k * jnp.exp(cl - cc)).astype(BF16)
            pmat_ref[rows, h * CHUNK:(h + 1) * CHUNK] = (_mm_nt(q, k) * gam).astype(BF16)
    ext_ref[0:CONV_PAD, :] = ext_ref[tile:tile + CONV_PAD, :]
    t_invs = _unit_lower_inverses(a_mats)
    for t_inv, rhs, (rows, hs) in zip(t_invs, rhss, where):
        sol = _mm_split(_split2(t_inv), rhs)
        upre_ref[rows, hs] = sol[:, 0:GDN_DV]
        wmix_ref[rows, hs] = sol[:, GDN_DV:2 * GDN_DV].astype(BF16)

    heads = range(GDN_HEADS)
    for c in range(tile // CHUNK):
        r0 = c * CHUNK
        rows = slice(r0, r0 + CHUNK)
        hss = [slice(h * GDN_DV, (h + 1) * GDN_DV) for h in heads]
        ms = [st_ref[h] for h in heads]
        us = [upre_ref[rows, hss[h]] - _mm(wmix_ref[rows, hss[h]], ms[h]) for h in heads]
        os = [_mm(qdec_ref[rows, hss[h]], ms[h])
              + _mm(pmat_ref[rows, h * CHUNK:(h + 1) * CHUNK], us[h]) for h in heads]
        for h in heads:
            cl = cum_col[r0 + CHUNK - 1:r0 + CHUNK, h:h + 1]
            st_ref[h] = jnp.exp(cl) * ms[h] + _mm_tn(kdec_ref[rows, hss[h]], us[h])
        for h in heads:
            o = _rms(os[h]) * ng_ref[...]
            gs = slice(GDN_QKV + h * GDN_DV, GDN_QKV + (h + 1) * GDN_DV)
            o_ref[rows, hss[h]] = (o * _silu(p_ref[rows, gs])).astype(o_ref.dtype)


def _gdn(p, abt, conv_w, a_log, dt_bias, ng, *, bsz, seq, tile):
    nt = seq // tile
    pad = lambda v: jnp.pad(v, (0, LANES - v.shape[0])).reshape(1, LANES)
    col = lambda v: jnp.pad(v, (0, GATE_ROWS - v.shape[0])).reshape(GATE_ROWS, 1)
    const = lambda shape: pl.BlockSpec(shape, lambda b, c: (0,) * len(shape))
    tri3, triu3 = _chunk_cumsum_matrices(tile)
    return pl.pallas_call(
        _gdn_kernel,
        out_shape=jax.ShapeDtypeStruct((bsz * seq, GDN_V), BF16),
        grid=(bsz, nt),
        in_specs=[
            pl.BlockSpec((tile, GDN_WIDTH), lambda b, c: (b * nt + c, 0)),
            pl.BlockSpec((GATE_ROWS, tile), lambda b, c: (0, b * nt + c)),
            const(tri3.shape), const(triu3.shape),
            const((GDN_CONV, GDN_QKV)),
            const((1, LANES)), const((1, LANES)),
            const((GATE_ROWS, 1)), const((GATE_ROWS, 1)),
            const((1, GDN_DV)),
        ],
        out_specs=pl.BlockSpec((tile, GDN_V), lambda b, c: (b * nt + c, 0)),
        scratch_shapes=[
            pltpu.VMEM((CONV_PAD + tile, GDN_QKV), F32),
            pltpu.VMEM((GDN_HEADS, GDN_DK, GDN_DV), F32),
            pltpu.VMEM((tile, GDN_V), F32),
            pltpu.VMEM((tile, GDN_V), BF16),
            pltpu.VMEM((tile, GDN_QK), BF16),
            pltpu.VMEM((tile, GDN_QK), BF16),
            pltpu.VMEM((tile, GDN_HEADS * CHUNK), BF16),
        ],
        compiler_params=_params("arbitrary", "arbitrary"),
        name="gdn_mixer",
    )(p, abt, tri3, triu3, conv_w, pad(a_log), pad(dt_bias), col(a_log), col(dt_bias),
      ng.reshape(1, GDN_DV))


def _merge_kernel(x_ref, ya_ref, yb_ref, yc_ref, mr_ref, bm_ref, wa_ref, wb_ref, wc_ref, wo_ref,
                  gm_ref, o_ref):
    d = x_ref.shape[1]
    merged = None
    for i, (y_ref, w_ref) in enumerate(((ya_ref, wa_ref), (yb_ref, wb_ref), (yc_ref, wc_ref))):
        gate = _sigmoid(mr_ref[:, i * d:(i + 1) * d] + bm_ref[:, i * d:(i + 1) * d])
        term = gate * _mm(y_ref[...], w_ref[...])
        merged = term if merged is None else merged + term
    o_ref[...] = x_ref[...] + gm_ref[...] * _mm(merged, wo_ref[...])


def _merge(x2, ya, yb, yc, mr, b_merge, wa, wb, wc, wo, gm, *, seq, tm):
    n, d = x2.shape
    per_batch = seq // tm
    row = lambda w: pl.BlockSpec((tm, w), lambda i: (i, 0))
    const = lambda shape: pl.BlockSpec(shape, lambda i: (0,) * len(shape))
    return pl.pallas_call(
        _merge_kernel,
        out_shape=jax.ShapeDtypeStruct((n, d), F32),
        grid=(n // tm,),
        in_specs=[
            row(d), row(ya.shape[1]), row(yb.shape[1]), row(yc.shape[1]), row(N_BRANCH * d),
            const((1, N_BRANCH * d)),
            const(wa.shape), const(wb.shape), const(wc.shape), const(wo.shape),
            pl.BlockSpec((None, 1, d), lambda i: (i // per_batch, 0, 0)),
        ],
        out_specs=row(d),
        compiler_params=_params("arbitrary"),
        name="merge_out",
    )(x2, ya, yb, yc, mr, b_merge.reshape(1, N_BRANCH * d), wa, wb, wc, wo, gm)


def _router_kernel(x_ref, sc_ref, sh_ref, g_ref, wr_ref, br_ref,
                   h_ref, idx_ref, gate_ref, rank_ref, cnt_ref, carry_ref):
    @pl.when(pl.program_id(0) == 0)
    def _():
        carry_ref[...] = jnp.zeros_like(carry_ref)

    tm = x_ref.shape[0]
    h = _norm_mod(x_ref[...], g_ref[...], sc_ref[...], sh_ref[...])
    _store_token_tiles(h_ref, h)
    logits = _mm_nt_f32(wr_ref[...], h) + br_ref[...]
    e_iota = lax.broadcasted_iota(I32, logits.shape, 0).astype(F32)
    vals, idxs, sels = [], [], []
    cur = logits
    for _ in range(TOP_K):
        m = jnp.max(cur, axis=0, keepdims=True)
        idx = jnp.min(jnp.where(cur == m, e_iota, float(N_EXPERTS)), axis=0, keepdims=True)
        sel = e_iota == idx
        cur = jnp.where(sel, -jnp.inf, cur)
        vals.append(m)
        idxs.append(idx.astype(I32))
        sels.append(sel)
    exps = [jnp.exp(v - vals[0]) for v in vals]
    denom = exps[0] + exps[1] + exps[2] + exps[3]
    onehot = jnp.zeros(logits.shape, F32)
    for sel in sels:
        onehot = onehot + jnp.where(sel, 1.0, 0.0)
    row = lax.broadcasted_iota(I32, (tm, tm), 0)
    col = lax.broadcasted_iota(I32, (tm, tm), 1)
    before = _mm(onehot, jnp.where(row < col, 1.0, 0.0)) + carry_ref[...]
    zeros_i = jnp.zeros((GATE_ROWS - TOP_K, tm), I32)
    idx_ref[...] = jnp.concatenate(idxs + [zeros_i], axis=0)
    gate_ref[...] = jnp.concatenate([e / denom for e in exps] + [zeros_i.astype(F32)], axis=0)
    ranks = [jnp.sum(jnp.where(sel, before, 0.0), axis=0, keepdims=True).astype(I32) for sel in sels]
    rank_ref[...] = jnp.concatenate(ranks + [zeros_i], axis=0)
    carry_ref[...] = carry_ref[...] + jnp.sum(onehot, axis=1, keepdims=True)
    cnt_ref[...] = jnp.broadcast_to(carry_ref[...], cnt_ref.shape).astype(I32)


def _router(x2, scale, shift, gain, w_router_t, b_router, *, seq, tm):
    n, d = x2.shape
    per_batch = seq // tm
    batch_spec = pl.BlockSpec((None, 1, d), lambda i: (i // per_batch, 0, 0))
    lane_out = pl.BlockSpec((GATE_ROWS, tm), lambda i: (0, i))
    return pl.pallas_call(
        _router_kernel,
        out_shape=[
            jax.ShapeDtypeStruct((n * SUBLANES, LANES), F32),
            jax.ShapeDtypeStruct((GATE_ROWS, n), I32),
            jax.ShapeDtypeStruct((GATE_ROWS, n), F32),
            jax.ShapeDtypeStruct((GATE_ROWS, n), I32),
            jax.ShapeDtypeStruct((N_EXPERTS, LANES), I32),
        ],
        grid=(n // tm,),
        in_specs=[
            pl.BlockSpec((tm, d), lambda i: (i, 0)),
            batch_spec, batch_spec,
            pl.BlockSpec((1, d), lambda i: (0, 0)),
            pl.BlockSpec((N_EXPERTS, d), lambda i: (0, 0)),
            pl.BlockSpec((N_EXPERTS, 1), lambda i: (0, 0)),
        ],
        out_specs=[
            pl.BlockSpec((tm * SUBLANES, LANES), lambda i: (i, 0)),
            lane_out, lane_out, lane_out,
            pl.BlockSpec((N_EXPERTS, LANES), lambda i: (0, 0)),
        ],
        scratch_shapes=[pltpu.VMEM((N_EXPERTS, 1), F32)],
        compiler_params=_params("arbitrary"),
        name="router",
    )(x2, scale, shift, gain, w_router_t, b_router.reshape(N_EXPERTS, 1))


def _store_token_tiles(ref, value):
    rows = value.shape[0]
    for s in range(SUBLANES):
        ref[pl.ds(s, rows, stride=SUBLANES), :] = value[:, s * LANES:(s + 1) * LANES]


def _load_token_tiles(ref, rows):
    return jnp.concatenate([ref[pl.ds(s, rows, stride=SUBLANES), :] for s in range(SUBLANES)],
                           axis=1)


def _tile_copy(src, src_row, dst, dst_row, sem):
    src_at = pl.multiple_of(src_row * SUBLANES, SUBLANES)
    dst_at = pl.multiple_of(dst_row * SUBLANES, SUBLANES)
    return pltpu.make_async_copy(src.at[pl.ds(src_at, SUBLANES), :],
                                 dst.at[pl.ds(dst_at, SUBLANES), :], sem)


def _load_positions(pos_hbm, pos_smem, sem, count):
    start = pl.multiple_of(pl.program_id(0) * count, count)
    cp = pltpu.make_async_copy(pos_hbm.at[pl.ds(start, count)], pos_smem, sem)
    cp.start()
    cp.wait()


def _dispatch_kernel(meta_ref, pos_hbm, h_ref, xs_hbm, pos_smem, zero_ref, pos_sem, row_sem):
    td = pos_smem.shape[0] // TOP_K
    n_blocks = xs_hbm.shape[0] // (MOE_BLOCK * SUBLANES)
    block_rows = MOE_BLOCK * SUBLANES

    @pl.when(pl.program_id(0) == 0)
    def _():
        zero_ref[...] = jnp.zeros_like(zero_ref)

        def fill(block):
            at = pl.multiple_of(block * block_rows, block_rows)
            return pltpu.make_async_copy(zero_ref, xs_hbm.at[pl.ds(at, block_rows), :], row_sem)

        def each_fill(act):
            for e in range(N_EXPERTS):
                @pl.when(meta_ref[e] > 0)
                def _():
                    act(fill(meta_ref[N_EXPERTS + e] // MOE_BLOCK - 1))

                @pl.when(meta_ref[2 * N_EXPERTS] + e < n_blocks)
                def _():
                    act(fill(meta_ref[2 * N_EXPERTS] + e))

        each_fill(lambda cp: cp.start())
        each_fill(lambda cp: cp.wait())

    _load_positions(pos_hbm, pos_smem, pos_sem, TOP_K * td)

    def start(g, carry):
        t0 = g * DMA_GROUP
        slots = [pos_smem[TOP_K * t0 + j] for j in range(TOP_K * DMA_GROUP)]
        for j, slot in enumerate(slots):
            _tile_copy(h_ref, t0 + j // TOP_K, xs_hbm, slot, row_sem).start(priority=j % 2)
        return carry

    def wait(g, carry):
        for _ in range(TOP_K * DMA_GROUP):
            _tile_copy(h_ref, 0, xs_hbm, 0, row_sem).wait()
        return carry

    lax.fori_loop(0, td // DMA_GROUP, start, 0)
    lax.fori_loop(0, td // DMA_GROUP, wait, 0)


def _dispatch(meta, pos_flat, h3, n_rows, *, td):
    n = h3.shape[0] // SUBLANES
    grid_spec = pltpu.PrefetchScalarGridSpec(
        num_scalar_prefetch=1,
        grid=(n // td,),
        in_specs=[pl.BlockSpec(memory_space=pl.ANY),
                  pl.BlockSpec((td * SUBLANES, LANES), lambda i, meta: (i, 0))],
        out_specs=pl.BlockSpec(memory_space=pl.ANY),
        scratch_shapes=[
            pltpu.SMEM((TOP_K * td,), I32),
            pltpu.VMEM((MOE_BLOCK * SUBLANES, LANES), F32),
            pltpu.SemaphoreType.DMA,
            pltpu.SemaphoreType.DMA,
        ],
    )
    return pl.pallas_call(
        _dispatch_kernel,
        out_shape=jax.ShapeDtypeStruct((n_rows * SUBLANES, LANES), F32),
        grid_spec=grid_spec,
        compiler_params=_params("arbitrary"),
        name="moe_dispatch",
    )(meta, pos_flat, h3)


PAIR_TILE = 2 * LANES
EXPERT_CHUNK = 256


def _pair_regroup_kernel(w_ref, o_ref):
    r = lax.broadcasted_iota(I32, (PAIR_TILE, PAIR_TILE), 0)
    c = lax.broadcasted_iota(I32, (PAIR_TILE, PAIR_TILE), 1)
    src = jnp.where(c < LANES, 2 * c, 2 * (c - LANES) + 1)
    perm = jnp.where(r == src, 1.0, 0.0).astype(BF16)
    for t in range(w_ref.shape[1] // PAIR_TILE):
        cols = slice(t * PAIR_TILE, (t + 1) * PAIR_TILE)
        o_ref[:, cols] = _mm(w_ref[:, cols], perm).astype(o_ref.dtype)


def _pair_regroup(w_all, layer, *, tr):
    _, e, d, width = w_all.shape
    return pl.pallas_call(
        _pair_regroup_kernel,
        out_shape=jax.ShapeDtypeStruct((e, d, width), BF16),
        grid=(e, d // tr),
        in_specs=[pl.BlockSpec((None, None, tr, width), lambda i, j: (layer, i, j, 0))],
        out_specs=pl.BlockSpec((None, tr, width), lambda i, j: (i, j, 0)),
        compiler_params=_params("arbitrary", "arbitrary"),
        name="gate_up_regroup",
    )(w_all)


def _expert_kernel(be_ref, x_ref, wgu_ref, bgu_ref, wd_ref, bd_ref, o_ref):
    f = wd_ref.shape[0]
    n_chunks = f // EXPERT_CHUNK
    gu_cols = 2 * EXPERT_CHUNK
    used = pl.program_id(0) < be_ref[pl.num_programs(0)]

    @pl.when(used)
    def _():
        x = _load_token_tiles(x_ref, MOE_BLOCK).astype(BF16)

        def gate_up(c):
            cols = slice(c * gu_cols, (c + 1) * gu_cols)
            return _mm(x, wgu_ref[:, cols]) + bgu_ref[:, cols]

        y = bd_ref[...]
        gu = gate_up(0)
        for c in range(n_chunks):
            gu_next = gate_up(c + 1) if c + 1 < n_chunks else None
            acts = []
            for t in range(EXPERT_CHUNK // LANES):
                gate = jnp.minimum(gu[:, t * PAIR_TILE:t * PAIR_TILE + LANES], SWIGLU_LIMIT)
                up = jnp.clip(gu[:, t * PAIR_TILE + LANES:(t + 1) * PAIR_TILE], -SWIGLU_LIMIT,
                              SWIGLU_LIMIT)
                acts.append(((up + 1.0) * gate * _sigmoid(SWIGLU_ALPHA * gate)).astype(BF16))
            y = y + _mm(jnp.concatenate(acts, axis=1),
                        wd_ref[c * EXPERT_CHUNK:(c + 1) * EXPERT_CHUNK, :])
            gu = gu_next
        _store_token_tiles(o_ref, y)

    @pl.when(jnp.logical_not(used))
    def _():
        o_ref[...] = jnp.zeros_like(o_ref)


def _experts(block_e, xs, wgu, bgu, wd_all, bd, layer):
    f, d = wd_all.shape[2:]
    block_rows = MOE_BLOCK * SUBLANES
    n_blocks = xs.shape[0] // block_rows
    grid_spec = pltpu.PrefetchScalarGridSpec(
        num_scalar_prefetch=1,
        grid=(n_blocks,),
        in_specs=[
            pl.BlockSpec((block_rows, LANES), lambda i, be: (i, 0)),
            pl.BlockSpec((None, d, 2 * f), lambda i, be: (be[i], 0, 0)),
            pl.BlockSpec((None, 1, 2 * f), lambda i, be: (be[i], 0, 0)),
            pl.BlockSpec((None, None, f, d), lambda i, be: (layer, be[i], 0, 0)),
            pl.BlockSpec((None, 1, d), lambda i, be: (be[i], 0, 0)),
        ],
        out_specs=pl.BlockSpec((block_rows, LANES), lambda i, be: (i, 0)),
    )
    return pl.pallas_call(
        _expert_kernel,
        out_shape=jax.ShapeDtypeStruct(xs.shape, F32),
        grid_spec=grid_spec,
        compiler_params=_params("arbitrary"),
        name="moe_experts",
    )(block_e, xs, wgu, bgu, wd_all, bd)


def _combine_kernel(pos_hbm, ys_hbm, gate_ref, x_ref, gf_ref, *rest, final_norm):
    if final_norm:
        ng_ref, o_ref, buf_ref, pos_smem, pos_sem, row_sem = rest
    else:
        o_ref, buf_ref, pos_smem, pos_sem, row_sem = rest
    tc = x_ref.shape[0]
    per_step = TOP_K * tc
    n_stage = pos_smem.shape[0] // per_step
    stage = lax.rem(pl.program_id(0), n_stage)

    @pl.when(stage == 0)
    def _():
        count = n_stage * per_step
        at = pl.multiple_of((pl.program_id(0) // n_stage) * count, count)
        cp = pltpu.make_async_copy(pos_hbm.at[pl.ds(at, count)], pos_smem, pos_sem)
        cp.start()
        cp.wait()

    base = stage * per_step
    half = tc // 2
    groups = half // DMA_GROUP

    def gather(part):
        sem = row_sem.at[part]

        def start(g, carry):
            t0 = part * half + g * DMA_GROUP
            slots = [pos_smem[base + TOP_K * t0 + j] for j in range(TOP_K * DMA_GROUP)]
            for j, slot in enumerate(slots):
                _tile_copy(ys_hbm, slot, buf_ref.at[j % TOP_K], t0 + j // TOP_K,
                           sem).start(priority=j % 2)
            return carry

        lax.fori_loop(0, groups, start, 0)

    def finish(part):
        sem = row_sem.at[part]

        def wait(g, carry):
            for _ in range(TOP_K * DMA_GROUP):
                _tile_copy(ys_hbm, 0, buf_ref.at[0], 0, sem).wait()
            return carry

        lax.fori_loop(0, groups, wait, 0)
        rows = slice(part * half, (part + 1) * half)
        gates = [jnp.broadcast_to(gate_ref[rows, k:k + 1], (half, LANES)) for k in range(TOP_K)]
        ys = []
        for s in range(SUBLANES):
            y = None
            for k in range(TOP_K):
                term = gates[k] * buf_ref[k, pl.ds(part * half * SUBLANES + s, half,
                                                   stride=SUBLANES), :]
                y = term if y is None else y + term
            ys.append(y)
        out = x_ref[rows, :] + gf_ref[...] * jnp.concatenate(ys, axis=1)
        if final_norm:
            out = _rms(out) * ng_ref[...]
        o_ref[rows, :] = out

    gather(0)
    gather(1)
    finish(0)
    finish(1)


def _combine(pos_flat, ys, gates, x2, gf, final_gain, *, seq, tc):
    n, d = x2.shape
    per_batch = seq // tc
    final_norm = final_gain is not None
    in_specs = [
        pl.BlockSpec(memory_space=pl.ANY),
        pl.BlockSpec(memory_space=pl.ANY),
        pl.BlockSpec((tc, TOP_K), lambda i: (i, 0)),
        pl.BlockSpec((tc, d), lambda i: (i, 0)),
        pl.BlockSpec((None, 1, d), lambda i: (i // per_batch, 0, 0)),
    ]
    args = [pos_flat, ys, gates, x2, gf]
    if final_norm:
        in_specs.append(pl.BlockSpec((1, d), lambda i: (0, 0)))
        args.append(final_gain.reshape(1, d))
    return pl.pallas_call(
        functools.partial(_combine_kernel, final_norm=final_norm),
        out_shape=jax.ShapeDtypeStruct((n, d), F32),
        grid=(n // tc,),
        in_specs=in_specs,
        out_specs=pl.BlockSpec((tc, d), lambda i: (i, 0)),
        scratch_shapes=[
            pltpu.VMEM((TOP_K, tc * SUBLANES, LANES), F32),
            pltpu.SMEM((math.gcd(n // tc, POS_STAGE) * TOP_K * tc,), I32),
            pltpu.SemaphoreType.DMA,
            pltpu.SemaphoreType.DMA((2,)),
        ],
        compiler_params=_params("arbitrary"),
        name="moe_combine",
    )(*args)


def _split_w_in(w_in):
    d = w_in.shape[0]
    sizes = (GLA_QK, GLA_QK, GLA_V, GLA_GATE_RANK, GLA_V, GDN_QKV, GDN_HEADS, GDN_HEADS, GDN_V,
             SSD_INNER, SSD_XBC, SSD_HEADS, N_BRANCH * d)
    parts, acc = [], 0
    for s in sizes:
        parts.append(w_in[:, acc:acc + s])
        acc += s
    (gq, gk, gv, glr, gr, dqkv, da, db, dg, sz, sxbc, sdt, mrg) = parts
    zeros = lambda w: jnp.zeros((d, w), w_in.dtype)
    w_gla = jnp.concatenate([gq, gk, gv, gr, glr, zeros(LANES - GLA_GATE_RANK)], axis=1)
    w_gdn = jnp.concatenate([dqkv, dg, da, db, zeros(LANES - 2 * GDN_HEADS)], axis=1)
    w_ssd = jnp.concatenate([sz, sxbc, sdt, zeros(LANES - SSD_HEADS)], axis=1)
    pad_rows = lambda m: jnp.pad(m, ((0, GATE_ROWS - m.shape[0]), (0, 0)))
    w_gdn_t = pad_rows(jnp.concatenate([da, db], axis=1).T)
    w_ssd_t = pad_rows(sdt.T)
    cast = lambda m: m.astype(BF16)
    return cast(w_gla), cast(w_gdn), cast(w_gdn_t), cast(w_ssd), cast(w_ssd_t), cast(mrg)


def _moe(x2, scale, shift, gain, gf, final_gain, layer, w_router, b_router, w_gate_up, b_gate_up,
         w_down, b_down, *, seq, tm):
    n, d = x2.shape
    h3, idx_t, gate_t, rank_t, counts = _router(x2, scale, shift, gain, w_router.T, b_router,
                                                seq=seq, tm=tm)
    counts = counts[:, 0]
    padded = ((counts + MOE_BLOCK - 1) // MOE_BLOCK) * MOE_BLOCK
    pad_end = jnp.cumsum(padded)
    pad_start = pad_end - padded
    n_blocks = -(-(n * TOP_K) // MOE_BLOCK) + N_EXPERTS
    n_used = pad_end[-1:] // MOE_BLOCK
    meta = jnp.concatenate([padded, pad_end, n_used]).astype(I32)
    block_start = jnp.arange(n_blocks, dtype=I32) * MOE_BLOCK
    block_e = jnp.minimum(jnp.sum((pad_end[None, :] <= block_start[:, None]).astype(I32), axis=1),
                          N_EXPERTS - 1)
    block_e = jnp.concatenate([block_e, n_used]).astype(I32)
    experts = jnp.arange(N_EXPERTS, dtype=I32)
    start_of = jnp.sum(jnp.where(idx_t[:TOP_K, :, None] == experts, pad_start, 0), axis=-1)
    pos = (start_of + rank_t[:TOP_K]).astype(I32)
    pos_flat = pos.T.reshape(-1)
    gates = gate_t[:TOP_K].T
    f = w_down.shape[2]
    wgu = _pair_regroup(w_gate_up, layer, tr=min(512, d))
    bgu = b_gate_up.reshape(N_EXPERTS, 2 * f // PAIR_TILE, LANES, 2).transpose(0, 1, 3, 2)
    xs = _dispatch(meta, pos_flat, h3, n_blocks * MOE_BLOCK, td=min(DISPATCH_TILE, n))
    ys = _experts(block_e, xs, wgu, bgu.reshape(N_EXPERTS, 1, 2 * f), w_down,
                  b_down.reshape(N_EXPERTS, 1, d), layer)
    return _combine(pos_flat, ys, gates, x2, gf, final_gain, seq=seq, tc=min(COMBINE_TILE, seq))


def kernel(x, c, w_mod, b_mod, norm_mix, norm_ffn, norm_final, w_in, gla_w_gate2, gla_b_gate2, gla_norm, gdn_conv_w, gdn_a_log, gdn_dt_bias, gdn_norm, ssd_conv_w, ssd_conv_b, ssd_a_log, ssd_dt_bias, ssd_d, ssd_norm, w_branch_gla, w_branch_gdn, w_branch_ssd, b_merge, w_out, w_router, b_router, w_gate_up, b_gate_up, w_down, b_down):
    bsz, seq, d = x.shape
    depth = w_mod.shape[0]
    n = bsz * seq
    tm = min(512, seq)
    mod = _modulation(c, w_mod, b_mod)
    x2 = x.reshape(n, d)
    for l in range(depth):
        sh_m, sc_m, g_m, sh_f, sc_f, g_f = [mod[l, :, i * d:(i + 1) * d].reshape(bsz, 1, d)
                                            for i in range(6)]
        w_gla, w_gdn, w_gdn_t, w_ssd, w_ssd_t, w_mrg = _split_w_in(w_in[l])
        gain = norm_mix[l].reshape(1, d)
        proj = functools.partial(_inproj, x2, sc_m, sh_m, gain, seq=seq, tm=tm)
        p_gla = proj(w_gla, None)
        p_gdn, abt = proj(w_gdn, w_gdn_t)
        p_ssd, dtt = proj(w_ssd, w_ssd_t)
        p_mrg = proj(w_mrg, None)
        w2 = jnp.pad(gla_w_gate2[l], ((0, LANES - GLA_GATE_RANK), (0, 0))).astype(BF16)
        tile = min(MIXER_TILE, seq)
        y_gla = _gla(p_gla, w2, gla_b_gate2[l].reshape(1, GLA_QK), gla_norm[l].reshape(1, GLA_DV),
                     bsz=bsz, seq=seq, tile=tile)
        y_gdn = _gdn(p_gdn, abt, gdn_conv_w[l], gdn_a_log[l], gdn_dt_bias[l], gdn_norm[l],
                     bsz=bsz, seq=seq, tile=tile)
        y_ssd = _ssd(p_ssd, dtt, ssd_conv_w[l], ssd_conv_b[l], ssd_a_log[l],
                     ssd_dt_bias[l], ssd_d[l], ssd_norm[l], bsz=bsz, seq=seq, tile=tile)
        x2 = _merge(x2, y_gla, y_gdn, y_ssd, p_mrg, b_merge[l],
                    w_branch_gla[l].astype(BF16), w_branch_gdn[l].astype(BF16),
                    w_branch_ssd[l].astype(BF16), w_out[l].astype(BF16), g_m, seq=seq, tm=tm)
        final_gain = norm_final if l == depth - 1 else None
        x2 = _moe(x2, sc_f, sh_f, norm_ffn[l].reshape(1, d), g_f, final_gain, l, w_router[l],
                  b_router[l], w_gate_up, b_gate_up[l], w_down, b_down[l], seq=seq, tm=tm)
    return x2.reshape(bsz, seq, d)
```

```python
import functools
import math

import jax
import jax.numpy as jnp
from jax import lax
from jax.experimental import pallas as pl
from jax.experimental.pallas import tpu as pltpu

F32 = jnp.float32
BF16 = jnp.bfloat16
I32 = jnp.int32
HIGHEST = lax.Precision.HIGHEST

CHUNK = 64
NORM_EPS = 1e-6
GLA_HEADS, GLA_DK, GLA_DV, GLA_GATE_RANK, GLA_GATE_TAU = 4, 64, 128, 16, 16.0
GDN_HEADS, GDN_DK, GDN_DV, GDN_CONV = 4, 128, 128, 4
SSD_HEADS, SSD_HEAD_DIM, SSD_GROUPS, SSD_STATE, SSD_CONV = 8, 64, 2, 64, 4
SSD_INNER = SSD_HEADS * SSD_HEAD_DIM
N_BRANCH = 3
N_EXPERTS, TOP_K = 32, 4
SWIGLU_LIMIT, SWIGLU_ALPHA = 7.0, 1.702
MOE_BLOCK = 512
COMBINE_TILE = 256
POS_STAGE = 8

LANES = 128
SUBLANES = 8
DMA_GROUP = 8
DISPATCH_TILE = 2048
CONV_PAD = 8
GATE_ROWS = 8
MIXER_TILE = 8 * CHUNK
VMEM_LIMIT = 48 * 1024 * 1024


def _params(*sem):
    return pltpu.CompilerParams(dimension_semantics=sem, vmem_limit_bytes=VMEM_LIMIT)


def _mm(a, b):
    return lax.dot_general(a.astype(BF16), b.astype(BF16), (((1,), (0,)), ((), ())),
                           preferred_element_type=F32)


def _mm_nt(a, b):
    return lax.dot_general(a.astype(BF16), b.astype(BF16), (((1,), (1,)), ((), ())),
                           preferred_element_type=F32)


def _mm_tn(a, b):
    return lax.dot_general(a.astype(BF16), b.astype(BF16), (((0,), (0,)), ((), ())),
                           preferred_element_type=F32)


def _mm_f32(a, b):
    return lax.dot_general(a, b, (((1,), (0,)), ((), ())), precision=HIGHEST,
                           preferred_element_type=F32)


def _mm_nt_f32(a, b):
    return lax.dot_general(a, b, (((1,), (1,)), ((), ())), precision=HIGHEST,
                           preferred_element_type=F32)


def _split2(x):
    hi = x.astype(BF16)
    return hi, (x - hi.astype(F32)).astype(BF16)


def _split3(x):
    hi = x.astype(BF16)
    r = x - hi.astype(F32)
    mid = r.astype(BF16)
    return hi, mid, (r - mid.astype(F32)).astype(BF16)


def _mm_split(a, b):
    (a_hi, a_lo), (b_hi, b_lo) = a, b
    m = a_hi.shape[0]
    both = _mm(jnp.concatenate([a_hi, a_lo], axis=0), b_hi)
    return both[:m] + both[m:] + _mm(a_hi, b_lo)


def _sigmoid(x):
    return 1.0 / (1.0 + jnp.exp(-x))


def _silu(x):
    return x * _sigmoid(x)


def _softplus(x):
    return jnp.maximum(x, 0.0) + jnp.log1p(jnp.exp(-jnp.abs(x)))


def _log_sigmoid(x):
    return jnp.minimum(x, 0.0) - jnp.log1p(jnp.exp(-jnp.abs(x)))


def _rms(x):
    return x * lax.rsqrt(jnp.mean(x * x, axis=-1, keepdims=True) + NORM_EPS)


def _norm_mod(x, gain, scale, shift):
    return _rms(x) * gain * (1.0 + scale) + shift


def _chunk_masks():
    row = lax.broadcasted_iota(I32, (CHUNK, CHUNK), 0)
    col = lax.broadcasted_iota(I32, (CHUNK, CHUNK), 1)
    return row >= col, row > col


def _masked_decay(cum_col, cum_row, mask):
    return jnp.where(mask, jnp.exp(jnp.where(mask, cum_col - cum_row, 0.0)), 0.0)


def _mod_kernel(c_ref, w_ref, b_ref, o_ref):
    o_ref[...] = _mm_f32(_silu(c_ref[...]), w_ref[...]) + b_ref[...]


def _modulation(c, w_mod, b_mod):
    depth, d, six_d = w_mod.shape
    bsz = c.shape[0]
    n_col = six_d // d
    return pl.pallas_call(
        _mod_kernel,
        out_shape=jax.ShapeDtypeStruct((depth, bsz, six_d), F32),
        grid=(depth, n_col),
        in_specs=[
            pl.BlockSpec((bsz, d), lambda l, j: (0, 0)),
            pl.BlockSpec((None, d, d), lambda l, j: (l, 0, j)),
            pl.BlockSpec((None, 1, d), lambda l, j: (l, 0, j)),
        ],
        out_specs=pl.BlockSpec((None, bsz, d), lambda l, j: (l, 0, j)),
        compiler_params=_params("arbitrary", "arbitrary"),
        name="adaln_mod",
    )(c, w_mod, b_mod.reshape(depth, 1, six_d))


def _inproj_kernel(x_ref, sc_ref, sh_ref, g_ref, w_ref, *rest, has_t):
    h = _norm_mod(x_ref[...], g_ref[...], sc_ref[...], sh_ref[...]).astype(BF16)
    if has_t:
        wt_ref, o_ref, ot_ref = rest
        ot_ref[...] = _mm_nt(wt_ref[...], h)
    else:
        (o_ref,) = rest
    o_ref[...] = _mm(h, w_ref[...])


def _inproj(x2, scale, shift, gain, w, wt, *, seq, tm):
    n, d = x2.shape
    width = w.shape[1]
    per_batch = seq // tm
    batch_spec = pl.BlockSpec((None, 1, d), lambda i: (i // per_batch, 0, 0))
    in_specs = [
        pl.BlockSpec((tm, d), lambda i: (i, 0)),
        batch_spec, batch_spec,
        pl.BlockSpec((1, d), lambda i: (0, 0)),
        pl.BlockSpec((d, width), lambda i: (0, 0)),
    ]
    args = [x2, scale, shift, gain, w]
    out_shape = [jax.ShapeDtypeStruct((n, width), F32)]
    out_specs = [pl.BlockSpec((tm, width), lambda i: (i, 0))]
    if wt is not None:
        in_specs.append(pl.BlockSpec((GATE_ROWS, d), lambda i: (0, 0)))
        args.append(wt)
        out_shape.append(jax.ShapeDtypeStruct((GATE_ROWS, n), F32))
        out_specs.append(pl.BlockSpec((GATE_ROWS, tm), lambda i: (0, i)))
    outs = pl.pallas_call(
        functools.partial(_inproj_kernel, has_t=wt is not None),
        out_shape=out_shape, grid=(n // tm,), in_specs=in_specs, out_specs=out_specs,
        compiler_params=_params("arbitrary"),
        name="inproj",
    )(*args)
    return outs if wt is not None else outs[0]


GLA_QK = GLA_HEADS * GLA_DK
GLA_V = GLA_HEADS * GLA_DV
GLA_WIDTH = 2 * GLA_QK + 2 * GLA_V + LANES


def _chunk_cumsum_matrices(tile):
    r = jnp.arange(tile)
    tri = ((r[:, None] // CHUNK == r[None, :] // CHUNK) & (r[:, None] >= r[None, :])).astype(BF16)
    tri3 = jnp.concatenate([tri, tri, tri], axis=1)
    return tri3, tri3.T


def _gla_kernel(p_ref, tri_ref, w2_ref, b2_ref, ng_ref, o_ref, st_ref, oin_ref, qd_ref):
    tile = p_ref.shape[0]
    heads = range(GLA_HEADS)

    @pl.when(pl.program_id(1) == 0)
    def _():
        st_ref[...] = jnp.zeros_like(st_ref)

    incl, _ = _chunk_masks()
    lr = p_ref[:, 2 * GLA_QK + 2 * GLA_V:GLA_WIDTH]
    gk = _log_sigmoid(_mm(lr, w2_ref[...]) + b2_ref[...]) / GLA_GATE_TAU
    bcum = _mm(tri_ref[...], jnp.concatenate(_split3(gk), axis=0))
    ks = [slice(h * GLA_DK, (h + 1) * GLA_DK) for h in heads]
    vs = [slice(h * GLA_DV, (h + 1) * GLA_DV) for h in heads]

    upds, decays = [], []
    for c in range(tile // CHUNK):
        rows = slice(c * CHUNK, (c + 1) * CHUNK)
        b = bcum[rows]
        b_last = b[CHUNK - 1:CHUNK, :]
        q_dec = (p_ref[rows, 0:GLA_QK] * (GLA_DK ** -0.5) * jnp.exp(b)).astype(BF16)
        k = p_ref[rows, GLA_QK:2 * GLA_QK]
        k_neg = (k * jnp.exp(-b)).astype(BF16)
        k_pos = (k * jnp.exp(b_last - b)).astype(BF16)
        v = [p_ref[rows, 2 * GLA_QK + h * GLA_DV:2 * GLA_QK + (h + 1) * GLA_DV].astype(BF16)
             for h in heads]
        atts = [jnp.where(incl, _mm_nt(q_dec[:, ks[h]], k_neg[:, ks[h]]), 0.0) for h in heads]
        for h in heads:
            oin_ref[rows, vs[h]] = _mm(atts[h], v[h])
        upds.append([_mm_tn(v[h], k_pos[:, ks[h]]) for h in heads])
        decays.append(jnp.exp(b_last))
        qd_ref[rows, :] = q_dec

    for c in range(tile // CHUNK):
        rows = slice(c * CHUNK, (c + 1) * CHUNK)
        sts = [st_ref[h] for h in heads]
        os = [oin_ref[rows, vs[h]] + _mm_nt(qd_ref[rows, ks[h]], sts[h]) for h in heads]
        for h in heads:
            st_ref[h] = sts[h] * decays[c][:, ks[h]] + upds[c][h]
        for h in heads:
            rs = slice(2 * GLA_QK + GLA_V + h * GLA_DV, 2 * GLA_QK + GLA_V + (h + 1) * GLA_DV)
            o = _rms(os[h]) * ng_ref[...]
            o_ref[rows, vs[h]] = (o * _silu(p_ref[rows, rs])).astype(o_ref.dtype)


def _gla(p, w2, b2, ng, *, bsz, seq, tile):
    nt = seq // tile
    tri3, _ = _chunk_cumsum_matrices(tile)
    const = lambda shape: pl.BlockSpec(shape, lambda b, c: (0,) * len(shape))
    return pl.pallas_call(
        _gla_kernel,
        out_shape=jax.ShapeDtypeStruct((bsz * seq, GLA_V), BF16),
        grid=(bsz, nt),
        in_specs=[
            pl.BlockSpec((tile, GLA_WIDTH), lambda b, c: (b * nt + c, 0)),
            const(tri3.shape), const((LANES, GLA_QK)), const((1, GLA_QK)), const((1, GLA_DV)),
        ],
        out_specs=pl.BlockSpec((tile, GLA_V), lambda b, c: (b * nt + c, 0)),
        scratch_shapes=[
            pltpu.VMEM((GLA_HEADS, GLA_DV, GLA_DK), F32),
            pltpu.VMEM((tile, GLA_V), F32),
            pltpu.VMEM((tile, GLA_QK), BF16),
        ],
        compiler_params=_params("arbitrary", "arbitrary"),
        name="gla_mixer",
    )(p, tri3, w2, b2, ng)


def _conv_window(ext_ref, w_ref, r0, taps):
    out = None
    for j in range(taps):
        start = CONV_PAD + r0 - (taps - 1) + j
        term = ext_ref[start:start + CHUNK, :] * w_ref[j:j + 1, :]
        out = term if out is None else out + term
    return out


SSD_BC = SSD_GROUPS * SSD_STATE
SSD_XBC = SSD_INNER + 2 * SSD_BC
SSD_WIDTH = SSD_INNER + SSD_XBC + LANES
SSD_HG = SSD_HEADS // SSD_GROUPS


def _ssd_kernel(p_ref, dtt_ref, tri_ref, triu_ref, expand_ref, cw_ref, cb_ref, alog_e_ref, dtb_ref,
                alog_c_ref, dtb_c_ref, dskip_ref, ng_ref, o_ref, ext_ref, st_ref, y_ref, xs_ref, cm_ref):
    tile = p_ref.shape[0]
    heads = range(SSD_HEADS)
    groups = range(SSD_GROUPS)
    gw = SSD_INNER // SSD_GROUPS

    @pl.when(pl.program_id(1) == 0)
    def _():
        st_ref[...] = jnp.zeros_like(st_ref)
        ext_ref[0:CONV_PAD, :] = jnp.zeros((CONV_PAD, SSD_XBC), F32)

    incl, _ = _chunk_masks()
    ext_ref[CONV_PAD:CONV_PAD + tile, :] = p_ref[:, SSD_INNER:SSD_INNER + SSD_XBC]
    dt_col = _softplus(p_ref[:, SSD_INNER + SSD_XBC:SSD_WIDTH] + dtb_ref[...])
    dt_e = _mm(jnp.concatenate(_split3(dt_col), axis=1), expand_ref[...])
    cum_e = _mm(tri_ref[...], jnp.concatenate(_split3(dt_e * (-jnp.exp(alog_e_ref[...]))), axis=0))
    dt_row = _softplus(dtt_ref[...] + dtb_c_ref[...])
    cum_row = _mm(jnp.concatenate(_split3(dt_row * (-jnp.exp(alog_c_ref[...]))), axis=1),
                  triu_ref[...])
    hs = [slice(h * SSD_HEAD_DIM, (h + 1) * SSD_HEAD_DIM) for h in heads]
    gs = [slice(g * gw, (g + 1) * gw) for g in groups]

    upds, st_decays = [], []
    for c in range(tile // CHUNK):
        r0 = c * CHUNK
        rows = slice(r0, r0 + CHUNK)
        xbc = _silu(_conv_window(ext_ref, cw_ref, r0, SSD_CONV) + cb_ref[...])
        xs = xbc[:, 0:SSD_INNER]
        xs_ref[rows, :] = xs
        cm_ref[rows, :] = xbc[:, SSD_INNER + SSD_BC:SSD_XBC].astype(BF16)
        bm = [xbc[:, SSD_INNER + g * SSD_STATE:SSD_INNER + (g + 1) * SSD_STATE] for g in groups]
        cm = [xbc[:, SSD_INNER + SSD_BC + g * SSD_STATE:SSD_INNER + SSD_BC + (g + 1) * SSD_STATE]
              for g in groups]
        ce = cum_e[rows]
        cle = ce[CHUNK - 1:CHUNK, :]
        xdt = xs * dt_e[rows]
        xdt_bf = xdt.astype(BF16)
        xdec = (xdt * jnp.exp(cle - ce)).astype(BF16)
        cb = [_mm_nt(cm[g], bm[g]) for g in groups]
        segs = [_masked_decay(ce[:, hs[h]], cum_row[h:h + 1, rows], incl) for h in heads]
        for h in heads:
            y_ref[rows, hs[h]] = _mm(cb[h // SSD_HG] * segs[h], xdt_bf[:, hs[h]])
        upds.append([_mm_tn(bm[g], xdec[:, gs[g]]) for g in groups])
        st_decays.append(jnp.exp(cle))
    ext_ref[0:CONV_PAD, :] = ext_ref[tile:tile + CONV_PAD, :]

    for c in range(tile // CHUNK):
        rows = slice(c * CHUNK, (c + 1) * CHUNK)
        sts = [st_ref[g] for g in groups]
        y_in = [_mm(cm_ref[rows, g * SSD_STATE:(g + 1) * SSD_STATE], sts[g]) for g in groups]
        for g in groups:
            st_ref[g] = st_decays[c][:, gs[g]] * sts[g] + upds[c][g]
        y = y_ref[rows, :] + jnp.concatenate(y_in, axis=1) * jnp.exp(cum_e[rows])
        y = (y + dskip_ref[...] * xs_ref[rows, :]) * _silu(p_ref[rows, 0:SSD_INNER])
        for g in groups:
            o_ref[rows, gs[g]] = (_rms(y[:, gs[g]]) * ng_ref[:, gs[g]]).astype(o_ref.dtype)


def _ssd(p, dtt, conv_w, conv_b, a_log, dt_bias, d_skip, ng, *, bsz, seq, tile):
    nt = seq // tile
    pad = lambda v: jnp.pad(v, (0, LANES - v.shape[0])).reshape(1, LANES)
    col = lambda v: v.reshape(SSD_HEADS, 1)
    const = lambda shape: pl.BlockSpec(shape, lambda b, c: (0,) * len(shape))
    tri3, triu3 = _chunk_cumsum_matrices(tile)
    per_channel = lambda v: jnp.repeat(v, SSD_HEAD_DIM).reshape(1, SSD_INNER)
    expand = (jnp.arange(LANES)[:, None] == jnp.arange(SSD_INNER)[None, :] // SSD_HEAD_DIM)
    expand3 = jnp.concatenate([expand.astype(BF16)] * 3, axis=0)
    return pl.pallas_call(
        _ssd_kernel,
        out_shape=jax.ShapeDtypeStruct((bsz * seq, SSD_INNER), BF16),
        grid=(bsz, nt),
        in_specs=[
            pl.BlockSpec((tile, SSD_WIDTH), lambda b, c: (b * nt + c, 0)),
            pl.BlockSpec((GATE_ROWS, tile), lambda b, c: (0, b * nt + c)),
            const(tri3.shape), const(triu3.shape), const(expand3.shape),
            const((SSD_CONV, SSD_XBC)), const((1, SSD_XBC)),
            const((1, SSD_INNER)), const((1, LANES)),
            const((SSD_HEADS, 1)), const((SSD_HEADS, 1)),
            const((1, SSD_INNER)), const((1, SSD_INNER)),
        ],
        out_specs=pl.BlockSpec((tile, SSD_INNER), lambda b, c: (b * nt + c, 0)),
        scratch_shapes=[
            pltpu.VMEM((CONV_PAD + tile, SSD_XBC), F32),
            pltpu.VMEM((SSD_GROUPS, SSD_STATE, SSD_INNER // SSD_GROUPS), F32),
            pltpu.VMEM((tile, SSD_INNER), F32),
            pltpu.VMEM((tile, SSD_INNER), F32),
            pltpu.VMEM((tile, SSD_BC), BF16),
        ],
        compiler_params=_params("arbitrary", "arbitrary"),
        name="ssd_mixer",
    )(p, dtt, tri3, triu3, expand3, conv_w, conv_b.reshape(1, SSD_XBC), per_channel(a_log),
      pad(dt_bias), col(a_log), col(dt_bias), per_channel(d_skip), ng.reshape(1, SSD_INNER))


GDN_QK = GDN_HEADS * GDN_DK
GDN_V = GDN_HEADS * GDN_DV
GDN_QKV = 2 * GDN_QK + GDN_V
GDN_WIDTH = GDN_QKV + GDN_V + LANES


def _l2norm(x):
    return x * lax.rsqrt(jnp.sum(x * x, axis=-1, keepdims=True) + 1e-6)


def _unit_lower_inverses(mats):
    shape = mats[0].shape
    eye = jnp.where(lax.broadcasted_iota(I32, shape, 0) == lax.broadcasted_iota(I32, shape, 1),
                    1.0, 0.0)
    ps = [eye - a for a in mats]
    n_splits = [_split2(-a) for a in mats]
    steps = (shape[0] - 1).bit_length() - 1
    for _ in range(steps):
        n_splits = [_split2(_mm_split(ns, ns)) for ns in n_splits]
        ps = [p + _mm_split(_split2(p), ns) for p, ns in zip(ps, n_splits)]
    return ps


def _gdn_kernel(p_ref, abt_ref, tri_ref, triu_ref, cw_ref, alog_ref, dtb_ref, alog_c_ref, dtb_c_ref,
                ng_ref, o_ref, ext_ref, st_ref, upre_ref, wmix_ref, qdec_ref, kdec_ref, pmat_ref):
    tile = p_ref.shape[0]

    @pl.when(pl.program_id(1) == 0)
    def _():
        st_ref[...] = jnp.zeros_like(st_ref)
        ext_ref[0:CONV_PAD, :] = jnp.zeros((CONV_PAD, GDN_QKV), F32)

    incl, strict = _chunk_masks()
    ext_ref[CONV_PAD:CONV_PAD + tile, :] = p_ref[:, 0:GDN_QKV]
    ab = p_ref[:, GDN_QKV + GDN_V:GDN_WIDTH]
    beta_all = _sigmoid(ab)
    g_col = -jnp.exp(alog_ref[...]) * _softplus(ab + dtb_ref[...])
    cum_col = _mm(tri_ref[...], jnp.concatenate(_split3(g_col), axis=0))
    g_row = -jnp.exp(alog_c_ref[...]) * _softplus(abt_ref[...] + dtb_c_ref[...])
    cum_row = _mm(jnp.concatenate(_split3(g_row), axis=1), triu_ref[...])

    a_mats, rhss, where = [], [], []
    for c in range(tile // CHUNK):
        r0 = c * CHUNK
        rows = slice(r0, r0 + CHUNK)
        qkv = _silu(_conv_window(ext_ref, cw_ref, r0, GDN_CONV))
        for h in range(GDN_HEADS):
            hs = slice(h * GDN_DV, (h + 1) * GDN_DV)
            q = _l2norm(qkv[:, h * GDN_DK:(h + 1) * GDN_DK]) * (GDN_DK ** -0.5)
            k = _l2norm(qkv[:, GDN_QK + h * GDN_DK:GDN_QK + (h + 1) * GDN_DK])
            v = qkv[:, 2 * GDN_QK + h * GDN_DV:2 * GDN_QK + (h + 1) * GDN_DV]
            beta = beta_all[rows, GDN_HEADS + h:GDN_HEADS + h + 1]
            cc = cum_col[rows, h:h + 1]
            cl = cum_col[r0 + CHUNK - 1:r0 + CHUNK, h:h + 1]
            gam = _masked_decay(cc, cum_row[h:h + 1, rows], incl)
            a_mats.append(jnp.where(strict, beta * _mm_nt(k, k) * gam, 0.0))
            rhss.append(_split2(jnp.concatenate([beta * v, (beta * jnp.exp(cc)) * k], axis=1)))
            where.append((rows, hs))
            qdec_ref[rows, hs] = (q * jnp.exp(cc)).astype(BF16)
            kdec_ref[rows, hs] = (k * jnp.exp(cl - cc)).astype(BF16)
            pmat_ref[rows, h * CHUNK:(h + 1) * CHUNK] = (_mm_nt(q, k) * gam).astype(BF16)
    ext_ref[0:CONV_PAD, :] = ext_ref[tile:tile + CONV_PAD, :]
    t_invs = _unit_lower_inverses(a_mats)
    for t_inv, rhs, (rows, hs) in zip(t_invs, rhss, where):
        sol = _mm_split(_split2(t_inv), rhs)
        upre_ref[rows, hs] = sol[:, 0:GDN_DV]
        wmix_ref[rows, hs] = sol[:, GDN_DV:2 * GDN_DV].astype(BF16)

    heads = range(GDN_HEADS)
    for c in range(tile // CHUNK):
        r0 = c * CHUNK
        rows = slice(r0, r0 + CHUNK)
        hss = [slice(h * GDN_DV, (h + 1) * GDN_DV) for h in heads]
        ms = [st_ref[h] for h in heads]
        us = [upre_ref[rows, hss[h]] - _mm(wmix_ref[rows, hss[h]], ms[h]) for h in heads]
        os = [_mm(qdec_ref[rows, hss[h]], ms[h])
              + _mm(pmat_ref[rows, h * CHUNK:(h + 1) * CHUNK], us[h]) for h in heads]
        for h in heads:
            cl = cum_col[r0 + CHUNK - 1:r0 + CHUNK, h:h + 1]
            st_ref[h] = jnp.exp(cl) * ms[h] + _mm_tn(kdec_ref[rows, hss[h]], us[h])
        for h in heads:
            o = _rms(os[h]) * ng_ref[...]
            gs = slice(GDN_QKV + h * GDN_DV, GDN_QKV + (h + 1) * GDN_DV)
            o_ref[rows, hss[h]] = (o * _silu(p_ref[rows, gs])).astype(o_ref.dtype)


def _gdn(p, abt, conv_w, a_log, dt_bias, ng, *, bsz, seq, tile):
    nt = seq // tile
    pad = lambda v: jnp.pad(v, (0, LANES - v.shape[0])).reshape(1, LANES)
    col = lambda v: jnp.pad(v, (0, GATE_ROWS - v.shape[0])).reshape(GATE_ROWS, 1)
    const = lambda shape: pl.BlockSpec(shape, lambda b, c: (0,) * len(shape))
    tri3, triu3 = _chunk_cumsum_matrices(tile)
    return pl.pallas_call(
        _gdn_kernel,
        out_shape=jax.ShapeDtypeStruct((bsz * seq, GDN_V), BF16),
        grid=(bsz, nt),
        in_specs=[
            pl.BlockSpec((tile, GDN_WIDTH), lambda b, c: (b * nt + c, 0)),
            pl.BlockSpec((GATE_ROWS, tile), lambda b, c: (0, b * nt + c)),
            const(tri3.shape), const(triu3.shape),
            const((GDN_CONV, GDN_QKV)),
            const((1, LANES)), const((1, LANES)),
            const((GATE_ROWS, 1)), const((GATE_ROWS, 1)),
            const((1, GDN_DV)),
        ],
        out_specs=pl.BlockSpec((tile, GDN_V), lambda b, c: (b * nt + c, 0)),
        scratch_shapes=[
            pltpu.VMEM((CONV_PAD + tile, GDN_QKV), F32),
            pltpu.VMEM((GDN_HEADS, GDN_DK, GDN_DV), F32),
            pltpu.VMEM((tile, GDN_V), F32),
            pltpu.VMEM((tile, GDN_V), BF16),
            pltpu.VMEM((tile, GDN_QK), BF16),
            pltpu.VMEM((tile, GDN_QK), BF16),
            pltpu.VMEM((tile, GDN_HEADS * CHUNK), BF16),
        ],
        compiler_params=_params("arbitrary", "arbitrary"),
        name="gdn_mixer",
    )(p, abt, tri3, triu3, conv_w, pad(a_log), pad(dt_bias), col(a_log), col(dt_bias),
      ng.reshape(1, GDN_DV))


def _merge_kernel(x_ref, ya_ref, yb_ref, yc_ref, mr_ref, bm_ref, wa_ref, wb_ref, wc_ref, wo_ref,
                  gm_ref, o_ref):
    d = x_ref.shape[1]
    merged = None
    for i, (y_ref, w_ref) in enumerate(((ya_ref, wa_ref), (yb_ref, wb_ref), (yc_ref, wc_ref))):
        gate = _sigmoid(mr_ref[:, i * d:(i + 1) * d] + bm_ref[:, i * d:(i + 1) * d])
        term = gate * _mm(y_ref[...], w_ref[...])
        merged = term if merged is None else merged + term
    o_ref[...] = x_ref[...] + gm_ref[...] * _mm(merged, wo_ref[...])


def _merge(x2, ya, yb, yc, mr, b_merge, wa, wb, wc, wo, gm, *, seq, tm):
    n, d = x2.shape
    per_batch = seq // tm
    row = lambda w: pl.BlockSpec((tm, w), lambda i: (i, 0))
    const = lambda shape: pl.BlockSpec(shape, lambda i: (0,) * len(shape))
    return pl.pallas_call(
        _merge_kernel,
        out_shape=jax.ShapeDtypeStruct((n, d), F32),
        grid=(n // tm,),
        in_specs=[
            row(d), row(ya.shape[1]), row(yb.shape[1]), row(yc.shape[1]), row(N_BRANCH * d),
            const((1, N_BRANCH * d)),
            const(wa.shape), const(wb.shape), const(wc.shape), const(wo.shape),
            pl.BlockSpec((None, 1, d), lambda i: (i // per_batch, 0, 0)),
        ],
        out_specs=row(d),
        compiler_params=_params("arbitrary"),
        name="merge_out",
    )(x2, ya, yb, yc, mr, b_merge.reshape(1, N_BRANCH * d), wa, wb, wc, wo, gm)


def _router_kernel(x_ref, sc_ref, sh_ref, g_ref, wr_ref, br_ref,
                   h_ref, idx_ref, gate_ref, rank_ref, cnt_ref, carry_ref):
    @pl.when(pl.program_id(0) == 0)
    def _():
        carry_ref[...] = jnp.zeros_like(carry_ref)

    tm = x_ref.shape[0]
    h = _norm_mod(x_ref[...], g_ref[...], sc_ref[...], sh_ref[...])
    _store_token_tiles(h_ref, h)
    logits = _mm_nt_f32(wr_ref[...], h) + br_ref[...]
    e_iota = lax.broadcasted_iota(I32, logits.shape, 0).astype(F32)
    vals, idxs, sels = [], [], []
    cur = logits
    for _ in range(TOP_K):
        m = jnp.max(cur, axis=0, keepdims=True)
        idx = jnp.min(jnp.where(cur == m, e_iota, float(N_EXPERTS)), axis=0, keepdims=True)
        sel = e_iota == idx
        cur = jnp.where(sel, -jnp.inf, cur)
        vals.append(m)
        idxs.append(idx.astype(I32))
        sels.append(sel)
    exps = [jnp.exp(v - vals[0]) for v in vals]
    denom = exps[0] + exps[1] + exps[2] + exps[3]
    onehot = jnp.zeros(logits.shape, F32)
    for sel in sels:
        onehot = onehot + jnp.where(sel, 1.0, 0.0)
    row = lax.broadcasted_iota(I32, (tm, tm), 0)
    col = lax.broadcasted_iota(I32, (tm, tm), 1)
    before = _mm(onehot, jnp.where(row < col, 1.0, 0.0)) + carry_ref[...]
    zeros_i = jnp.zeros((GATE_ROWS - TOP_K, tm), I32)
    idx_ref[...] = jnp.concatenate(idxs + [zeros_i], axis=0)
    gate_ref[...] = jnp.concatenate([e / denom for e in exps] + [zeros_i.astype(F32)], axis=0)
    ranks = [jnp.sum(jnp.where(sel, before, 0.0), axis=0, keepdims=True).astype(I32) for sel in sels]
    rank_ref[...] = jnp.concatenate(ranks + [zeros_i], axis=0)
    carry_ref[...] = carry_ref[...] + jnp.sum(onehot, axis=1, keepdims=True)
    cnt_ref[...] = jnp.broadcast_to(carry_ref[...], cnt_ref.shape).astype(I32)


def _router(x2, scale, shift, gain, w_router_t, b_router, *, seq, tm):
    n, d = x2.shape
    per_batch = seq // tm
    batch_spec = pl.BlockSpec((None, 1, d), lambda i: (i // per_batch, 0, 0))
    lane_out = pl.BlockSpec((GATE_ROWS, tm), lambda i: (0, i))
    return pl.pallas_call(
        _router_kernel,
        out_shape=[
            jax.ShapeDtypeStruct((n * SUBLANES, LANES), F32),
            jax.ShapeDtypeStruct((GATE_ROWS, n), I32),
            jax.ShapeDtypeStruct((GATE_ROWS, n), F32),
            jax.ShapeDtypeStruct((GATE_ROWS, n), I32),
            jax.ShapeDtypeStruct((N_EXPERTS, LANES), I32),
        ],
        grid=(n // tm,),
        in_specs=[
            pl.BlockSpec((tm, d), lambda i: (i, 0)),
            batch_spec, batch_spec,
            pl.BlockSpec((1, d), lambda i: (0, 0)),
            pl.BlockSpec((N_EXPERTS, d), lambda i: (0, 0)),
            pl.BlockSpec((N_EXPERTS, 1), lambda i: (0, 0)),
        ],
        out_specs=[
            pl.BlockSpec((tm * SUBLANES, LANES), lambda i: (i, 0)),
            lane_out, lane_out, lane_out,
            pl.BlockSpec((N_EXPERTS, LANES), lambda i: (0, 0)),
        ],
        scratch_shapes=[pltpu.VMEM((N_EXPERTS, 1), F32)],
        compiler_params=_params("arbitrary"),
        name="router",
    )(x2, scale, shift, gain, w_router_t, b_router.reshape(N_EXPERTS, 1))


def _store_token_tiles(ref, value):
    rows = value.shape[0]
    for s in range(SUBLANES):
        ref[pl.ds(s, rows, stride=SUBLANES), :] = value[:, s * LANES:(s + 1) * LANES]


def _load_token_tiles(ref, rows):
    return jnp.concatenate([ref[pl.ds(s, rows, stride=SUBLANES), :] for s in range(SUBLANES)],
                           axis=1)


def _tile_copy(src, src_row, dst, dst_row, sem):
    src_at = pl.multiple_of(src_row * SUBLANES, SUBLANES)
    dst_at = pl.multiple_of(dst_row * SUBLANES, SUBLANES)
    return pltpu.make_async_copy(src.at[pl.ds(src_at, SUBLANES), :],
                                 dst.at[pl.ds(dst_at, SUBLANES), :], sem)


def _load_positions(pos_hbm, pos_smem, sem, count):
    start = pl.multiple_of(pl.program_id(0) * count, count)
    cp = pltpu.make_async_copy(pos_hbm.at[pl.ds(start, count)], pos_smem, sem)
    cp.start()
    cp.wait()


def _dispatch_kernel(meta_ref, pos_hbm, h_ref, xs_hbm, pos_smem, zero_ref, pos_sem, row_sem):
    td = pos_smem.shape[0] // TOP_K
    n_blocks = xs_hbm.shape[0] // (MOE_BLOCK * SUBLANES)
    block_rows = MOE_BLOCK * SUBLANES

    @pl.when(pl.program_id(0) == 0)
    def _():
        zero_ref[...] = jnp.zeros_like(zero_ref)

        def fill(block):
            at = pl.multiple_of(block * block_rows, block_rows)
            return pltpu.make_async_copy(zero_ref, xs_hbm.at[pl.ds(at, block_rows), :], row_sem)

        def each_fill(act):
            for e in range(N_EXPERTS):
                @pl.when(meta_ref[e] > 0)
                def _():
                    act(fill(meta_ref[N_EXPERTS + e] // MOE_BLOCK - 1))

                @pl.when(meta_ref[2 * N_EXPERTS] + e < n_blocks)
                def _():
                    act(fill(meta_ref[2 * N_EXPERTS] + e))

        each_fill(lambda cp: cp.start())
        each_fill(lambda cp: cp.wait())

    _load_positions(pos_hbm, pos_smem, pos_sem, TOP_K * td)

    def start(g, carry):
        t0 = g * DMA_GROUP
        slots = [pos_smem[TOP_K * t0 + j] for j in range(TOP_K * DMA_GROUP)]
        for j, slot in enumerate(slots):
            _tile_copy(h_ref, t0 + j // TOP_K, xs_hbm, slot, row_sem).start(priority=j % 2)
        return carry

    def wait(g, carry):
        for _ in range(TOP_K * DMA_GROUP):
            _tile_copy(h_ref, 0, xs_hbm, 0, row_sem).wait()
        return carry

    lax.fori_loop(0, td // DMA_GROUP, start, 0)
    lax.fori_loop(0, td // DMA_GROUP, wait, 0)


def _dispatch(meta, pos_flat, h3, n_rows, *, td):
    n = h3.shape[0] // SUBLANES
    grid_spec = pltpu.PrefetchScalarGridSpec(
        num_scalar_prefetch=1,
        grid=(n // td,),
        in_specs=[pl.BlockSpec(memory_space=pl.ANY),
                  pl.BlockSpec((td * SUBLANES, LANES), lambda i, meta: (i, 0))],
        out_specs=pl.BlockSpec(memory_space=pl.ANY),
        scratch_shapes=[
            pltpu.SMEM((TOP_K * td,), I32),
            pltpu.VMEM((MOE_BLOCK * SUBLANES, LANES), F32),
            pltpu.SemaphoreType.DMA,
            pltpu.SemaphoreType.DMA,
        ],
    )
    return pl.pallas_call(
        _dispatch_kernel,
        out_shape=jax.ShapeDtypeStruct((n_rows * SUBLANES, LANES), F32),
        grid_spec=grid_spec,
        compiler_params=_params("arbitrary"),
        name="moe_dispatch",
    )(meta, pos_flat, h3)


PAIR_TILE = 2 * LANES
EXPERT_CHUNK = 256


def _pair_regroup_kernel(w_ref, o_ref):
    r = lax.broadcasted_iota(I32, (PAIR_TILE, PAIR_TILE), 0)
    c = lax.broadcasted_iota(I32, (PAIR_TILE, PAIR_TILE), 1)
    src = jnp.where(c < LANES, 2 * c, 2 * (c - LANES) + 1)
    perm = jnp.where(r == src, 1.0, 0.0).astype(BF16)
    for t in range(w_ref.shape[1] // PAIR_TILE):
        cols = slice(t * PAIR_TILE, (t + 1) * PAIR_TILE)
        o_ref[:, cols] = _mm(w_ref[:, cols], perm).astype(o_ref.dtype)


def _pair_regroup(w_all, layer, *, tr):
    _, e, d, width = w_all.shape
    return pl.pallas_call(
        _pair_regroup_kernel,
        out_shape=jax.ShapeDtypeStruct((e, d, width), BF16),
        grid=(e, d // tr),
        in_specs=[pl.BlockSpec((None, None, tr, width), lambda i, j: (layer, i, j, 0))],
        out_specs=pl.BlockSpec((None, tr, width), lambda i, j: (i, j, 0)),
        compiler_params=_params("arbitrary", "arbitrary"),
        name="gate_up_regroup",
    )(w_all)


def _expert_kernel(be_ref, x_ref, wgu_ref, bgu_ref, wd_ref, bd_ref, o_ref):
    f = wd_ref.shape[0]
    n_chunks = f // EXPERT_CHUNK
    gu_cols = 2 * EXPERT_CHUNK
    used = pl.program_id(0) < be_ref[pl.num_programs(0)]

    @pl.when(used)
    def _():
        x = _load_token_tiles(x_ref, MOE_BLOCK).astype(BF16)

        def gate_up(c):
            cols = slice(c * gu_cols, (c + 1) * gu_cols)
            return _mm(x, wgu_ref[:, cols]) + bgu_ref[:, cols]

        y = bd_ref[...]
        gu = gate_up(0)
        for c in range(n_chunks):
            gu_next = gate_up(c + 1) if c + 1 < n_chunks else None
            acts = []
            for t in range(EXPERT_CHUNK // LANES):
                gate = jnp.minimum(gu[:, t * PAIR_TILE:t * PAIR_TILE + LANES], SWIGLU_LIMIT)
                up = jnp.clip(gu[:, t * PAIR_TILE + LANES:(t + 1) * PAIR_TILE], -SWIGLU_LIMIT,
                              SWIGLU_LIMIT)
                acts.append(((up + 1.0) * gate * _sigmoid(SWIGLU_ALPHA * gate)).astype(BF16))
            y = y + _mm(jnp.concatenate(acts, axis=1),
                        wd_ref[c * EXPERT_CHUNK:(c + 1) * EXPERT_CHUNK, :])
            gu = gu_next
        _store_token_tiles(o_ref, y)

    @pl.when(jnp.logical_not(used))
    def _():
        o_ref[...] = jnp.zeros_like(o_ref)


def _experts(block_e, xs, wgu, bgu, wd_all, bd, layer):
    f, d = wd_all.shape[2:]
    block_rows = MOE_BLOCK * SUBLANES
    n_blocks = xs.shape[0] // block_rows
    grid_spec = pltpu.PrefetchScalarGridSpec(
        num_scalar_prefetch=1,
        grid=(n_blocks,),
        in_specs=[
            pl.BlockSpec((block_rows, LANES), lambda i, be: (i, 0)),
            pl.BlockSpec((None, d, 2 * f), lambda i, be: (be[i], 0, 0)),
            pl.BlockSpec((None, 1, 2 * f), lambda i, be: (be[i], 0, 0)),
            pl.BlockSpec((None, None, f, d), lambda i, be: (layer, be[i], 0, 0)),
            pl.BlockSpec((None, 1, d), lambda i, be: (be[i], 0, 0)),
        ],
        out_specs=pl.BlockSpec((block_rows, LANES), lambda i, be: (i, 0)),
    )
    return pl.pallas_call(
        _expert_kernel,
        out_shape=jax.ShapeDtypeStruct(xs.shape, F32),
        grid_spec=grid_spec,
        compiler_params=_params("arbitrary"),
        name="moe_experts",
    )(block_e, xs, wgu, bgu, wd_all, bd)


def _combine_kernel(pos_hbm, ys_hbm, gate_ref, x_ref, gf_ref, *rest, final_norm):
    if final_norm:
        ng_ref, o_ref, buf_ref, pos_smem, pos_sem, row_sem = rest
    else:
        o_ref, buf_ref, pos_smem, pos_sem, row_sem = rest
    tc = x_ref.shape[0]
    per_step = TOP_K * tc
    n_stage = pos_smem.shape[0] // per_step
    stage = lax.rem(pl.program_id(0), n_stage)

    @pl.when(stage == 0)
    def _():
        count = n_stage * per_step
        at = pl.multiple_of((pl.program_id(0) // n_stage) * count, count)
        cp = pltpu.make_async_copy(pos_hbm.at[pl.ds(at, count)], pos_smem, pos_sem)
        cp.start()
        cp.wait()

    base = stage * per_step
    half = tc // 2
    groups = half // DMA_GROUP

    def gather(part):
        sem = row_sem.at[part]

        def start(g, carry):
            t0 = part * half + g * DMA_GROUP
            slots = [pos_smem[base + TOP_K * t0 + j] for j in range(TOP_K * DMA_GROUP)]
            for j, slot in enumerate(slots):
                _tile_copy(ys_hbm, slot, buf_ref.at[j % TOP_K], t0 + j // TOP_K,
                           sem).start(priority=j % 2)
            return carry

        lax.fori_loop(0, groups, start, 0)

    def finish(part):
        sem = row_sem.at[part]

        def wait(g, carry):
            for _ in range(TOP_K * DMA_GROUP):
                _tile_copy(ys_hbm, 0, buf_ref.at[0], 0, sem).wait()
            return carry

        lax.fori_loop(0, groups, wait, 0)
        rows = slice(part * half, (part + 1) * half)
        gates = [jnp.broadcast_to(gate_ref[rows, k:k + 1], (half, LANES)) for k in range(TOP_K)]
        ys = []
        for s in range(SUBLANES):
            y = None
            for k in range(TOP_K):
                term = gates[k] * buf_ref[k, pl.ds(part * half * SUBLANES + s, half,
                                                   stride=SUBLANES), :]
                y = term if y is None else y + term
            ys.append(y)
        out = x_ref[rows, :] + gf_ref[...] * jnp.concatenate(ys, axis=1)
        if final_norm:
            out = _rms(out) * ng_ref[...]
        o_ref[rows, :] = out

    gather(0)
    gather(1)
    finish(0)
    finish(1)


def _combine(pos_flat, ys, gates, x2, gf, final_gain, *, seq, tc):
    n, d = x2.shape
    per_batch = seq // tc
    final_norm = final_gain is not None
    in_specs = [
        pl.BlockSpec(memory_space=pl.ANY),
        pl.BlockSpec(memory_space=pl.ANY),
        pl.BlockSpec((tc, TOP_K), lambda i: (i, 0)),
        pl.BlockSpec((tc, d), lambda i: (i, 0)),
        pl.BlockSpec((None, 1, d), lambda i: (i // per_batch, 0, 0)),
    ]
    args = [pos_flat, ys, gates, x2, gf]
    if final_norm:
        in_specs.append(pl.BlockSpec((1, d), lambda i: (0, 0)))
        args.append(final_gain.reshape(1, d))
    return pl.pallas_call(
        functools.partial(_combine_kernel, final_norm=final_norm),
        out_shape=jax.ShapeDtypeStruct((n, d), F32),
        grid=(n // tc,),
        in_specs=in_specs,
        out_specs=pl.BlockSpec((tc, d), lambda i: (i, 0)),
        scratch_shapes=[
            pltpu.VMEM((TOP_K, tc * SUBLANES, LANES), F32),
            pltpu.SMEM((math.gcd(n // tc, POS_STAGE) * TOP_K * tc,), I32),
            pltpu.SemaphoreType.DMA,
            pltpu.SemaphoreType.DMA((2,)),
        ],
        compiler_params=_params("arbitrary"),
        name="moe_combine",
    )(*args)


def _split_w_in(w_in):
    d = w_in.shape[0]
    sizes = (GLA_QK, GLA_QK, GLA_V, GLA_GATE_RANK, GLA_V, GDN_QKV, GDN_HEADS, GDN_HEADS, GDN_V,
             SSD_INNER, SSD_XBC, SSD_HEADS, N_BRANCH * d)
    parts, acc = [], 0
    for s in sizes:
        parts.append(w_in[:, acc:acc + s])
        acc += s
    (gq, gk, gv, glr, gr, dqkv, da, db, dg, sz, sxbc, sdt, mrg) = parts
    zeros = lambda w: jnp.zeros((d, w), w_in.dtype)
    w_gla = jnp.concatenate([gq, gk, gv, gr, glr, zeros(LANES - GLA_GATE_RANK)], axis=1)
    w_gdn = jnp.concatenate([dqkv, dg, da, db, zeros(LANES - 2 * GDN_HEADS)], axis=1)
    w_ssd = jnp.concatenate([sz, sxbc, sdt, zeros(LANES - SSD_HEADS)], axis=1)
    pad_rows = lambda m: jnp.pad(m, ((0, GATE_ROWS - m.shape[0]), (0, 0)))
    w_gdn_t = pad_rows(jnp.concatenate([da, db], axis=1).T)
    w_ssd_t = pad_rows(sdt.T)
    cast = lambda m: m.astype(BF16)
    return cast(w_gla), cast(w_gdn), cast(w_gdn_t), cast(w_ssd), cast(w_ssd_t), cast(mrg)


def _moe(x2, scale, shift, gain, gf, final_gain, layer, w_router, b_router, w_gate_up, b_gate_up,
         w_down, b_down, *, seq, tm):
    n, d = x2.shape
    h3, idx_t, gate_t, rank_t, counts = _router(x2, scale, shift, gain, w_router.T, b_router,
                                                seq=seq, tm=tm)
    counts = counts[:, 0]
    padded = ((counts + MOE_BLOCK - 1) // MOE_BLOCK) * MOE_BLOCK
    pad_end = jnp.cumsum(padded)
    pad_start = pad_end - padded
    n_blocks = -(-(n * TOP_K) // MOE_BLOCK) + N_EXPERTS
    n_used = pad_end[-1:] // MOE_BLOCK
    meta = jnp.concatenate([padded, pad_end, n_used]).astype(I32)
    block_start = jnp.arange(n_blocks, dtype=I32) * MOE_BLOCK
    block_e = jnp.minimum(jnp.sum((pad_end[None, :] <= block_start[:, None]).astype(I32), axis=1),
                          N_EXPERTS - 1)
    block_e = jnp.concatenate([block_e, n_used]).astype(I32)
    experts = jnp.arange(N_EXPERTS, dtype=I32)
    start_of = jnp.sum(jnp.where(idx_t[:TOP_K, :, None] == experts, pad_start, 0), axis=-1)
    pos = (start_of + rank_t[:TOP_K]).astype(I32)
    pos_flat = pos.T.reshape(-1)
    gates = gate_t[:TOP_K].T
    f = w_down.shape[2]
    wgu = _pair_regroup(w_gate_up, layer, tr=min(512, d))
    bgu = b_gate_up.reshape(N_EXPERTS, 2 * f // PAIR_TILE, LANES, 2).transpose(0, 1, 3, 2)
    xs = _dispatch(meta, pos_flat, h3, n_blocks * MOE_BLOCK, td=min(DISPATCH_TILE, n))
    ys = _experts(block_e, xs, wgu, bgu.reshape(N_EXPERTS, 1, 2 * f), w_down,
                  b_down.reshape(N_EXPERTS, 1, d), layer)
    return _combine(pos_flat, ys, gates, x2, gf, final_gain, seq=seq, tc=min(COMBINE_TILE, seq))


def kernel(x, c, w_mod, b_mod, norm_mix, norm_ffn, norm_final, w_in, gla_w_gate2, gla_b_gate2, gla_norm, gdn_conv_w, gdn_a_log, gdn_dt_bias, gdn_norm, ssd_conv_w, ssd_conv_b, ssd_a_log, ssd_dt_bias, ssd_d, ssd_norm, w_branch_gla, w_branch_gdn, w_branch_ssd, b_merge, w_out, w_router, b_router, w_gate_up, b_gate_up, w_down, b_down):
    bsz, seq, d = x.shape
    depth = w_mod.shape[0]
    n = bsz * seq
    tm = min(512, seq)
    mod = _modulation(c, w_mod, b_mod)
    x2 = x.reshape(n, d)
    for l in range(depth):
        sh_m, sc_m, g_m, sh_f, sc_f, g_f = [mod[l, :, i * d:(i + 1) * d].reshape(bsz, 1, d)
                                            for i in range(6)]
        w_gla, w_gdn, w_gdn_t, w_ssd, w_ssd_t, w_mrg = _split_w_in(w_in[l])
        gain = norm_mix[l].reshape(1, d)
        proj = functools.partial(_inproj, x2, sc_m, sh_m, gain, seq=seq, tm=tm)
        p_gla = proj(w_gla, None)
        p_gdn, abt = proj(w_gdn, w_gdn_t)
        p_ssd, dtt = proj(w_ssd, w_ssd_t)
        p_mrg = proj(w_mrg, None)
        w2 = jnp.pad(gla_w_gate2[l], ((0, LANES - GLA_GATE_RANK), (0, 0))).astype(BF16)
        tile = min(MIXER_TILE, seq)
        y_gla = _gla(p_gla, w2, gla_b_gate2[l].reshape(1, GLA_QK), gla_norm[l].reshape(1, GLA_DV),
                     bsz=bsz, seq=seq, tile=tile)
        y_gdn = _gdn(p_gdn, abt, gdn_conv_w[l], gdn_a_log[l], gdn_dt_bias[l], gdn_norm[l],
                     bsz=bsz, seq=seq, tile=tile)
        y_ssd = _ssd(p_ssd, dtt, ssd_conv_w[l], ssd_conv_b[l], ssd_a_log[l],
                     ssd_dt_bias[l], ssd_d[l], ssd_norm[l], bsz=bsz, seq=seq, tile=tile)
        x2 = _merge(x2, y_gla, y_gdn, y_ssd, p_mrg, b_merge[l],
                    w_branch_gla[l].astype(BF16), w_branch_gdn[l].astype(BF16),
                    w_branch_ssd[l].astype(BF16), w_out[l].astype(BF16), g_m, seq=seq, tm=tm)
        final_gain = norm_final if l == depth - 1 else None
        x2 = _moe(x2, sc_f, sh_f, norm_ffn[l].reshape(1, d), g_f, final_gain, l, w_router[l],
                  b_router[l], w_gate_up, b_gate_up[l], w_down, b_down[l], seq=seq, tm=tm)
    return x2.reshape(bsz, seq, d)
```

```python
import functools
import math

import jax
import jax.numpy as jnp
from jax import lax
from jax.experimental import pallas as pl
from jax.experimental.pallas import tpu as pltpu

F32 = jnp.float32
BF16 = jnp.bfloat16
I32 = jnp.int32
HIGHEST = lax.Precision.HIGHEST

CHUNK = 64
NORM_EPS = 1e-6
GLA_HEADS, GLA_DK, GLA_DV, GLA_GATE_RANK, GLA_GATE_TAU = 4, 64, 128, 16, 16.0
GDN_HEADS, GDN_DK, GDN_DV, GDN_CONV = 4, 128, 128, 4
SSD_HEADS, SSD_HEAD_DIM, SSD_GROUPS, SSD_STATE, SSD_CONV = 8, 64, 2, 64, 4
SSD_INNER = SSD_HEADS * SSD_HEAD_DIM
N_BRANCH = 3
N_EXPERTS, TOP_K = 32, 4
SWIGLU_LIMIT, SWIGLU_ALPHA = 7.0, 1.702
MOE_BLOCK = 512
COMBINE_TILE = 256
POS_STAGE = 8

LANES = 128
SUBLANES = 8
DMA_GROUP = 8
DISPATCH_TILE = 2048
CONV_PAD = 8
GATE_ROWS = 8
MIXER_TILE = 8 * CHUNK
VMEM_LIMIT = 48 * 1024 * 1024


def _params(*sem):
    return pltpu.CompilerParams(dimension_semantics=sem, vmem_limit_bytes=VMEM_LIMIT)


def _mm(a, b):
    return lax.dot_general(a.astype(BF16), b.astype(BF16), (((1,), (0,)), ((), ())),
                           preferred_element_type=F32)


def _mm_nt(a, b):
    return lax.dot_general(a.astype(BF16), b.astype(BF16), (((1,), (1,)), ((), ())),
                           preferred_element_type=F32)


def _mm_tn(a, b):
    return lax.dot_general(a.astype(BF16), b.astype(BF16), (((0,), (0,)), ((), ())),
                           preferred_element_type=F32)


def _mm_f32(a, b):
    return lax.dot_general(a, b, (((1,), (0,)), ((), ())), precision=HIGHEST,
                           preferred_element_type=F32)


def _mm_nt_f32(a, b):
    return lax.dot_general(a, b, (((1,), (1,)), ((), ())), precision=HIGHEST,
                           preferred_element_type=F32)


def _split2(x):
    hi = x.astype(BF16)
    return hi, (x - hi.astype(F32)).astype(BF16)


def _split3(x):
    hi = x.astype(BF16)
    r = x - hi.astype(F32)
    mid = r.astype(BF16)
    return hi, mid, (r - mid.astype(F32)).astype(BF16)


def _mm_split(a, b):
    (a_hi, a_lo), (b_hi, b_lo) = a, b
    m = a_hi.shape[0]
    both = _mm(jnp.concatenate([a_hi, a_lo], axis=0), b_hi)
    return both[:m] + both[m:] + _mm(a_hi, b_lo)


def _sigmoid(x):
    return 1.0 / (1.0 + jnp.exp(-x))


def _silu(x):
    return x * _sigmoid(x)


def _softplus(x):
    return jnp.maximum(x, 0.0) + jnp.log1p(jnp.exp(-jnp.abs(x)))


def _log_sigmoid(x):
    return jnp.minimum(x, 0.0) - jnp.log1p(jnp.exp(-jnp.abs(x)))


def _rms(x):
    return x * lax.rsqrt(jnp.mean(x * x, axis=-1, keepdims=True) + NORM_EPS)


def _norm_mod(x, gain, scale, shift):
    return _rms(x) * gain * (1.0 + scale) + shift


def _chunk_masks():
    row = lax.broadcasted_iota(I32, (CHUNK, CHUNK), 0)
    col = lax.broadcasted_iota(I32, (CHUNK, CHUNK), 1)
    return row >= col, row > col


def _masked_decay(cum_col, cum_row, mask):
    return jnp.where(mask, jnp.exp(jnp.where(mask, cum_col - cum_row, 0.0)), 0.0)


def _mod_kernel(c_ref, w_ref, b_ref, o_ref):
    o_ref[...] = _mm_f32(_silu(c_ref[...]), w_ref[...]) + b_ref[...]


def _modulation(c, w_mod, b_mod):
    depth, d, six_d = w_mod.shape
    bsz = c.shape[0]
    n_col = six_d // d
    return pl.pallas_call(
        _mod_kernel,
        out_shape=jax.ShapeDtypeStruct((depth, bsz, six_d), F32),
        grid=(depth, n_col),
        in_specs=[
            pl.BlockSpec((bsz, d), lambda l, j: (0, 0)),
            pl.BlockSpec((None, d, d), lambda l, j: (l, 0, j)),
            pl.BlockSpec((None, 1, d), lambda l, j: (l, 0, j)),
        ],
        out_specs=pl.BlockSpec((None, bsz, d), lambda l, j: (l, 0, j)),
        compiler_params=_params("arbitrary", "arbitrary"),
        name="adaln_mod",
    )(c, w_mod, b_mod.reshape(depth, 1, six_d))


def _inproj_kernel(x_ref, sc_ref, sh_ref, g_ref, w_ref, *rest, has_t):
    h = _norm_mod(x_ref[...], g_ref[...], sc_ref[...], sh_ref[...]).astype(BF16)
    if has_t:
        wt_ref, o_ref, ot_ref = rest
        ot_ref[...] = _mm_nt(wt_ref[...], h)
    else:
        (o_ref,) = rest
    o_ref[...] = _mm(h, w_ref[...])


def _inproj(x2, scale, shift, gain, w, wt, *, seq, tm):
    n, d = x2.shape
    width = w.shape[1]
    per_batch = seq // tm
    batch_spec = pl.BlockSpec((None, 1, d), lambda i: (i // per_batch, 0, 0))
    in_specs = [
        pl.BlockSpec((tm, d), lambda i: (i, 0)),
        batch_spec, batch_spec,
        pl.BlockSpec((1, d), lambda i: (0, 0)),
        pl.BlockSpec((d, width), lambda i: (0, 0)),
    ]
    args = [x2, scale, shift, gain, w]
    out_shape = [jax.ShapeDtypeStruct((n, width), F32)]
    out_specs = [pl.BlockSpec((tm, width), lambda i: (i, 0))]
    if wt is not None:
        in_specs.append(pl.BlockSpec((GATE_ROWS, d), lambda i: (0, 0)))
        args.append(wt)
        out_shape.append(jax.ShapeDtypeStruct((GATE_ROWS, n), F32))
        out_specs.append(pl.BlockSpec((GATE_ROWS, tm), lambda i: (0, i)))
    outs = pl.pallas_call(
        functools.partial(_inproj_kernel, has_t=wt is not None),
        out_shape=out_shape, grid=(n // tm,), in_specs=in_specs, out_specs=out_specs,
        compiler_params=_params("arbitrary"),
        name="inproj",
    )(*args)
    return outs if wt is not None else outs[0]


GLA_QK = GLA_HEADS * GLA_DK
GLA_V = GLA_HEADS * GLA_DV
GLA_WIDTH = 2 * GLA_QK + 2 * GLA_V + LANES


def _chunk_cumsum_matrices(tile):
    r = jnp.arange(tile)
    tri = ((r[:, None] // CHUNK == r[None, :] // CHUNK) & (r[:, None] >= r[None, :])).astype(BF16)
    tri3 = jnp.concatenate([tri, tri, tri], axis=1)
    return tri3, tri3.T


def _gla_kernel(p_ref, tri_ref, w2_ref, b2_ref, ng_ref, o_ref, st_ref, oin_ref, qd_ref):
    tile = p_ref.shape[0]
    heads = range(GLA_HEADS)

    @pl.when(pl.program_id(1) == 0)
    def _():
        st_ref[...] = jnp.zeros_like(st_ref)

    incl, _ = _chunk_masks()
    lr = p_ref[:, 2 * GLA_QK + 2 * GLA_V:GLA_WIDTH]
    gk = _log_sigmoid(_mm(lr, w2_ref[...]) + b2_ref[...]) / GLA_GATE_TAU
    bcum = _mm(tri_ref[...], jnp.concatenate(_split3(gk), axis=0))
    ks = [slice(h * GLA_DK, (h + 1) * GLA_DK) for h in heads]
    vs = [slice(h * GLA_DV, (h + 1) * GLA_DV) for h in heads]

    upds, decays = [], []
    for c in range(tile // CHUNK):
        rows = slice(c * CHUNK, (c + 1) * CHUNK)
        b = bcum[rows]
        b_last = b[CHUNK - 1:CHUNK, :]
        q_dec = (p_ref[rows, 0:GLA_QK] * (GLA_DK ** -0.5) * jnp.exp(b)).astype(BF16)
        k = p_ref[rows, GLA_QK:2 * GLA_QK]
        k_neg = (k * jnp.exp(-b)).astype(BF16)
        k_pos = (k * jnp.exp(b_last - b)).astype(BF16)
        v = [p_ref[rows, 2 * GLA_QK + h * GLA_DV:2 * GLA_QK + (h + 1) * GLA_DV].astype(BF16)
             for h in heads]
        atts = [jnp.where(incl, _mm_nt(q_dec[:, ks[h]], k_neg[:, ks[h]]), 0.0) for h in heads]
        for h in heads:
            oin_ref[rows, vs[h]] = _mm(atts[h], v[h])
        upds.append([_mm_tn(v[h], k_pos[:, ks[h]]) for h in heads])
        decays.append(jnp.exp(b_last))
        qd_ref[rows, :] = q_dec

    for c in range(tile // CHUNK):
        rows = slice(c * CHUNK, (c + 1) * CHUNK)
        sts = [st_ref[h] for h in heads]
        os = [oin_ref[rows, vs[h]] + _mm_nt(qd_ref[rows, ks[h]], sts[h]) for h in heads]
        for h in heads:
            st_ref[h] = sts[h] * decays[c][:, ks[h]] + upds[c][h]
        for h in heads:
            rs = slice(2 * GLA_QK + GLA_V + h * GLA_DV, 2 * GLA_QK + GLA_V + (h + 1) * GLA_DV)
            o = _rms(os[h]) * ng_ref[...]
            o_ref[rows, vs[h]] = (o * _silu(p_ref[rows, rs])).astype(o_ref.dtype)


def _gla(p, w2, b2, ng, *, bsz, seq, tile):
    nt = seq // tile
    tri3, _ = _chunk_cumsum_matrices(tile)
    const = lambda shape: pl.BlockSpec(shape, lambda b, c: (0,) * len(shape))
    return pl.pallas_call(
        _gla_kernel,
        out_shape=jax.ShapeDtypeStruct((bsz * seq, GLA_V), BF16),
        grid=(bsz, nt),
        in_specs=[
            pl.BlockSpec((tile, GLA_WIDTH), lambda b, c: (b * nt + c, 0)),
            const(tri3.shape), const((LANES, GLA_QK)), const((1, GLA_QK)), const((1, GLA_DV)),
        ],
        out_specs=pl.BlockSpec((tile, GLA_V), lambda b, c: (b * nt + c, 0)),
        scratch_shapes=[
            pltpu.VMEM((GLA_HEADS, GLA_DV, GLA_DK), F32),
            pltpu.VMEM((tile, GLA_V), F32),
            pltpu.VMEM((tile, GLA_QK), BF16),
        ],
        compiler_params=_params("arbitrary", "arbitrary"),
        name="gla_mixer",
    )(p, tri3, w2, b2, ng)


def _conv_window(ext_ref, w_ref, r0, taps):
    window = ext_ref[r0:r0 + CONV_PAD + CHUNK, :]
    out = None
    for j in range(taps):
        shift = taps - 1 - j
        rows = pltpu.roll(window, shift, 0) if shift else window
        term = rows[CONV_PAD:, :] * w_ref[j:j + 1, :]
        out = term if out is None else out + term
    return out


SSD_BC = SSD_GROUPS * SSD_STATE
SSD_XBC = SSD_INNER + 2 * SSD_BC
SSD_WIDTH = SSD_INNER + SSD_XBC + LANES
SSD_HG = SSD_HEADS // SSD_GROUPS


def _ssd_kernel(p_ref, dtt_ref, tri_ref, triu_ref, expand_ref, cw_ref, cb_ref, alog_e_ref, dtb_ref,
                alog_c_ref, dtb_c_ref, dskip_ref, ng_ref, o_ref, ext_ref, st_ref, y_ref, xs_ref, cm_ref):
    tile = p_ref.shape[0]
    heads = range(SSD_HEADS)
    groups = range(SSD_GROUPS)
    gw = SSD_INNER // SSD_GROUPS

    @pl.when(pl.program_id(1) == 0)
    def _():
        st_ref[...] = jnp.zeros_like(st_ref)
        ext_ref[0:CONV_PAD, :] = jnp.zeros((CONV_PAD, SSD_XBC), F32)

    incl, _ = _chunk_masks()
    ext_ref[CONV_PAD:CONV_PAD + tile, :] = p_ref[:, SSD_INNER:SSD_INNER + SSD_XBC]
    dt_col = _softplus(p_ref[:, SSD_INNER + SSD_XBC:SSD_WIDTH] + dtb_ref[...])
    dt_e = _mm(jnp.concatenate(_split3(dt_col), axis=1), expand_ref[...])
    cum_e = _mm(tri_ref[...], jnp.concatenate(_split3(dt_e * (-jnp.exp(alog_e_ref[...]))), axis=0))
    dt_row = _softplus(dtt_ref[...] + dtb_c_ref[...])
    cum_row = _mm(jnp.concatenate(_split3(dt_row * (-jnp.exp(alog_c_ref[...]))), axis=1),
                  triu_ref[...])
    hs = [slice(h * SSD_HEAD_DIM, (h + 1) * SSD_HEAD_DIM) for h in heads]
    gs = [slice(g * gw, (g + 1) * gw) for g in groups]

    upds, st_decays = [], []
    for c in range(tile // CHUNK):
        r0 = c * CHUNK
        rows = slice(r0, r0 + CHUNK)
        xbc = _silu(_conv_window(ext_ref, cw_ref, r0, SSD_CONV) + cb_ref[...])
        xs = xbc[:, 0:SSD_INNER]
        xs_ref[rows, :] = xs
        cm_ref[rows, :] = xbc[:, SSD_INNER + SSD_BC:SSD_XBC].astype(BF16)
        bm = [xbc[:, SSD_INNER + g * SSD_STATE:SSD_INNER + (g + 1) * SSD_STATE] for g in groups]
        cm = [xbc[:, SSD_INNER + SSD_BC + g * SSD_STATE:SSD_INNER + SSD_BC + (g + 1) * SSD_STATE]
              for g in groups]
        ce = cum_e[rows]
        cle = ce[CHUNK - 1:CHUNK, :]
        xdt = xs * dt_e[rows]
        xdt_bf = xdt.astype(BF16)
        xdec = (xdt * jnp.exp(cle - ce)).astype(BF16)
        cb = [_mm_nt(cm[g], bm[g]) for g in groups]
        segs = [_masked_decay(ce[:, hs[h]], cum_row[h:h + 1, rows], incl) for h in heads]
        for h in heads:
            y_ref[rows, hs[h]] = _mm(cb[h // SSD_HG] * segs[h], xdt_bf[:, hs[h]])
        upds.append([_mm_tn(bm[g], xdec[:, gs[g]]) for g in groups])
        st_decays.append(jnp.exp(cle))
    ext_ref[0:CONV_PAD, :] = ext_ref[tile:tile + CONV_PAD, :]

    for c in range(tile // CHUNK):
        rows = slice(c * CHUNK, (c + 1) * CHUNK)
        sts = [st_ref[g] for g in groups]
        y_in = [_mm(cm_ref[rows, g * SSD_STATE:(g + 1) * SSD_STATE], sts[g]) for g in groups]
        for g in groups:
            st_ref[g] = st_decays[c][:, gs[g]] * sts[g] + upds[c][g]
        y = y_ref[rows, :] + jnp.concatenate(y_in, axis=1) * jnp.exp(cum_e[rows])
        y = (y + dskip_ref[...] * xs_ref[rows, :]) * _silu(p_ref[rows, 0:SSD_INNER])
        for g in groups:
            o_ref[rows, gs[g]] = (_rms(y[:, gs[g]]) * ng_ref[:, gs[g]]).astype(o_ref.dtype)


def _ssd(p, dtt, conv_w, conv_b, a_log, dt_bias, d_skip, ng, *, bsz, seq, tile):
    nt = seq // tile
    pad = lambda v: jnp.pad(v, (0, LANES - v.shape[0])).reshape(1, LANES)
    col = lambda v: v.reshape(SSD_HEADS, 1)
    const = lambda shape: pl.BlockSpec(shape, lambda b, c: (0,) * len(shape))
    tri3, triu3 = _chunk_cumsum_matrices(tile)
    per_channel = lambda v: jnp.repeat(v, SSD_HEAD_DIM).reshape(1, SSD_INNER)
    expand = (jnp.arange(LANES)[:, None] == jnp.arange(SSD_INNER)[None, :] // SSD_HEAD_DIM)
    expand3 = jnp.concatenate([expand.astype(BF16)] * 3, axis=0)
    return pl.pallas_call(
        _ssd_kernel,
        out_shape=jax.ShapeDtypeStruct((bsz * seq, SSD_INNER), BF16),
        grid=(bsz, nt),
        in_specs=[
            pl.BlockSpec((tile, SSD_WIDTH), lambda b, c: (b * nt + c, 0)),
            pl.BlockSpec((GATE_ROWS, tile), lambda b, c: (0, b * nt + c)),
            const(tri3.shape), const(triu3.shape), const(expand3.shape),
            const((SSD_CONV, SSD_XBC)), const((1, SSD_XBC)),
            const((1, SSD_INNER)), const((1, LANES)),
            const((SSD_HEADS, 1)), const((SSD_HEADS, 1)),
            const((1, SSD_INNER)), const((1, SSD_INNER)),
        ],
        out_specs=pl.BlockSpec((tile, SSD_INNER), lambda b, c: (b * nt + c, 0)),
        scratch_shapes=[
            pltpu.VMEM((CONV_PAD + tile, SSD_XBC), F32),
            pltpu.VMEM((SSD_GROUPS, SSD_STATE, SSD_INNER // SSD_GROUPS), F32),
            pltpu.VMEM((tile, SSD_INNER), F32),
            pltpu.VMEM((tile, SSD_INNER), F32),
            pltpu.VMEM((tile, SSD_BC), BF16),
        ],
        compiler_params=_params("arbitrary", "arbitrary"),
        name="ssd_mixer",
    )(p, dtt, tri3, triu3, expand3, conv_w, conv_b.reshape(1, SSD_XBC), per_channel(a_log),
      pad(dt_bias), col(a_log), col(dt_bias), per_channel(d_skip), ng.reshape(1, SSD_INNER))


GDN_QK = GDN_HEADS * GDN_DK
GDN_V = GDN_HEADS * GDN_DV
GDN_QKV = 2 * GDN_QK + GDN_V
GDN_WIDTH = GDN_QKV + GDN_V + LANES


def _l2norm(x):
    return x * lax.rsqrt(jnp.sum(x * x, axis=-1, keepdims=True) + 1e-6)


def _unit_lower_inverses(mats):
    shape = mats[0].shape
    eye = jnp.where(lax.broadcasted_iota(I32, shape, 0) == lax.broadcasted_iota(I32, shape, 1),
                    1.0, 0.0)
    ps = [eye - a for a in mats]
    n_splits = [_split2(-a) for a in mats]
    steps = (shape[0] - 1).bit_length() - 1
    for _ in range(steps):
        n_splits = [_split2(_mm_split(ns, ns)) for ns in n_splits]
        ps = [p + _mm_split(_split2(p), ns) for p, ns in zip(ps, n_splits)]
    return ps


def _gdn_kernel(p_ref, abt_ref, tri_ref, triu_ref, cw_ref, alog_ref, dtb_ref, alog_c_ref, dtb_c_ref,
                ng_ref, o_ref, ext_ref, st_ref, upre_ref, wmix_ref, qdec_ref, kdec_ref, pmat_ref):
    tile = p_ref.shape[0]

    @pl.when(pl.program_id(1) == 0)
    def _():
        st_ref[...] = jnp.zeros_like(st_ref)
        ext_ref[0:CONV_PAD, :] = jnp.zeros((CONV_PAD, GDN_QKV), F32)

    incl, strict = _chunk_masks()
    ext_ref[CONV_PAD:CONV_PAD + tile, :] = p_ref[:, 0:GDN_QKV]
    ab = p_ref[:, GDN_QKV + GDN_V:GDN_WIDTH]
    beta_all = _sigmoid(ab)
    g_col = -jnp.exp(alog_ref[...]) * _softplus(ab + dtb_ref[...])
    cum_col = _mm(tri_ref[...], jnp.concatenate(_split3(g_col), axis=0))
    g_row = -jnp.exp(alog_c_ref[...]) * _softplus(abt_ref[...] + dtb_c_ref[...])
    cum_row = _mm(jnp.concatenate(_split3(g_row), axis=1), triu_ref[...])

    a_mats, rhss, where = [], [], []
    for c in range(tile // CHUNK):
        r0 = c * CHUNK
        rows = slice(r0, r0 + CHUNK)
        qkv = _silu(_conv_window(ext_ref, cw_ref, r0, GDN_CONV))
        for h in range(GDN_HEADS):
            hs = slice(h * GDN_DV, (h + 1) * GDN_DV)
            q = _l2norm(qkv[:, h * GDN_DK:(h + 1) * GDN_DK]) * (GDN_DK ** -0.5)
            k = _l2norm(qkv[:, GDN_QK + h * GDN_DK:GDN_QK + (h + 1) * GDN_DK])
            v = qkv[:, 2 * GDN_QK + h * GDN_DV:2 * GDN_QK + (h + 1) * GDN_DV]
            beta = beta_all[rows, GDN_HEADS + h:GDN_HEADS + h + 1]
            cc = cum_col[rows, h:h + 1]
            cl = cum_col[r0 + CHUNK - 1:r0 + CHUNK, h:h + 1]
            gam = _masked_decay(cc, cum_row[h:h + 1, rows], incl)
            a_mats.append(jnp.where(strict, beta * _mm_nt(k, k) * gam, 0.0))
            rhss.append(_split2(jnp.concatenate([beta * v, (beta * jnp.exp(cc)) * k], axis=1)))
            where.append((rows, hs))
            qdec_ref[rows, hs] = (q * jnp.exp(cc)).astype(BF16)
            kdec_ref[rows, hs] = (k * jnp.exp(cl - cc)).astype(BF16)
            pmat_ref[rows, h * CHUNK:(h + 1) * CHUNK] = (_mm_nt(q, k) * gam).astype(BF16)
    ext_ref[0:CONV_PAD, :] = ext_ref[tile:tile + CONV_PAD, :]
    t_invs = _unit_lower_inverses(a_mats)
    for t_inv, rhs, (rows, hs) in zip(t_invs, rhss, where):
        sol = _mm_split(_split2(t_inv), rhs)
        upre_ref[rows, hs] = sol[:, 0:GDN_DV]
        wmix_ref[rows, hs] = sol[:, GDN_DV:2 * GDN_DV].astype(BF16)

    heads = range(GDN_HEADS)
    for c in range(tile // CHUNK):
        r0 = c * CHUNK
        rows = slice(r0, r0 + CHUNK)
        hss = [slice(h * GDN_DV, (h + 1) * GDN_DV) for h in heads]
        ms = [st_ref[h] for h in heads]
        us = [upre_ref[rows, hss[h]] - _mm(wmix_ref[rows, hss[h]], ms[h]) for h in heads]
        os = [_mm(qdec_ref[rows, hss[h]], ms[h])
              + _mm(pmat_ref[rows, h * CHUNK:(h + 1) * CHUNK], us[h]) for h in heads]
        for h in heads:
            cl = cum_col[r0 + CHUNK - 1:r0 + CHUNK, h:h + 1]
            st_ref[h] = jnp.exp(cl) * ms[h] + _mm_tn(kdec_ref[rows, hss[h]], us[h])
        for h in heads:
            o = _rms(os[h]) * ng_ref[...]
            gs = slice(GDN_QKV + h * GDN_DV, GDN_QKV + (h + 1) * GDN_DV)
            o_ref[rows, hss[h]] = (o * _silu(p_ref[rows, gs])).astype(o_ref.dtype)


def _gdn(p, abt, conv_w, a_log, dt_bias, ng, *, bsz, seq, tile):
    nt = seq // tile
    pad = lambda v: jnp.pad(v, (0, LANES - v.shape[0])).reshape(1, LANES)
    col = lambda v: jnp.pad(v, (0, GATE_ROWS - v.shape[0])).reshape(GATE_ROWS, 1)
    const = lambda shape: pl.BlockSpec(shape, lambda b, c: (0,) * len(shape))
    tri3, triu3 = _chunk_cumsum_matrices(tile)
    return pl.pallas_call(
        _gdn_kernel,
        out_shape=jax.ShapeDtypeStruct((bsz * seq, GDN_V), BF16),
        grid=(bsz, nt),
        in_specs=[
            pl.BlockSpec((tile, GDN_WIDTH), lambda b, c: (b * nt + c, 0)),
            pl.BlockSpec((GATE_ROWS, tile), lambda b, c: (0, b * nt + c)),
            const(tri3.shape), const(triu3.shape),
            const((GDN_CONV, GDN_QKV)),
            const((1, LANES)), const((1, LANES)),
            const((GATE_ROWS, 1)), const((GATE_ROWS, 1)),
            const((1, GDN_DV)),
        ],
        out_specs=pl.BlockSpec((tile, GDN_V), lambda b, c: (b * nt + c, 0)),
        scratch_shapes=[
            pltpu.VMEM((CONV_PAD + tile, GDN_QKV), F32),
            pltpu.VMEM((GDN_HEADS, GDN_DK, GDN_DV), F32),
            pltpu.VMEM((tile, GDN_V), F32),
            pltpu.VMEM((tile, GDN_V), BF16),
            pltpu.VMEM((tile, GDN_QK), BF16),
            pltpu.VMEM((tile, GDN_QK), BF16),
            pltpu.VMEM((tile, GDN_HEADS * CHUNK), BF16),
        ],
        compiler_params=_params("arbitrary", "arbitrary"),
        name="gdn_mixer",
    )(p, abt, tri3, triu3, conv_w, pad(a_log), pad(dt_bias), col(a_log), col(dt_bias),
      ng.reshape(1, GDN_DV))


def _merge_kernel(x_ref, ya_ref, yb_ref, yc_ref, mr_ref, bm_ref, wa_ref, wb_ref, wc_ref, wo_ref,
                  gm_ref, o_ref):
    d = x_ref.shape[1]
    merged = None
    for i, (y_ref, w_ref) in enumerate(((ya_ref, wa_ref), (yb_ref, wb_ref), (yc_ref, wc_ref))):
        gate = _sigmoid(mr_ref[:, i * d:(i + 1) * d] + bm_ref[:, i * d:(i + 1) * d])
        term = gate * _mm(y_ref[...], w_ref[...])
        merged = term if merged is None else merged + term
    o_ref[...] = x_ref[...] + gm_ref[...] * _mm(merged, wo_ref[...])


def _merge(x2, ya, yb, yc, mr, b_merge, wa, wb, wc, wo, gm, *, seq, tm):
    n, d = x2.shape
    per_batch = seq // tm
    row = lambda w: pl.BlockSpec((tm, w), lambda i: (i, 0))
    const = lambda shape: pl.BlockSpec(shape, lambda i: (0,) * len(shape))
    return pl.pallas_call(
        _merge_kernel,
        out_shape=jax.ShapeDtypeStruct((n, d), F32),
        grid=(n // tm,),
        in_specs=[
            row(d), row(ya.shape[1]), row(yb.shape[1]), row(yc.shape[1]), row(N_BRANCH * d),
            const((1, N_BRANCH * d)),
            const(wa.shape), const(wb.shape), const(wc.shape), const(wo.shape),
            pl.BlockSpec((None, 1, d), lambda i: (i // per_batch, 0, 0)),
        ],
        out_specs=row(d),
        compiler_params=_params("arbitrary"),
        name="merge_out",
    )(x2, ya, yb, yc, mr, b_merge.reshape(1, N_BRANCH * d), wa, wb, wc, wo, gm)


def _router_kernel(x_ref, sc_ref, sh_ref, g_ref, wr_ref, br_ref,
                   h_ref, idx_ref, gate_ref, rank_ref, cnt_ref, carry_ref):
    @pl.when(pl.program_id(0) == 0)
    def _():
        carry_ref[...] = jnp.zeros_like(carry_ref)

    tm = x_ref.shape[0]
    h = _norm_mod(x_ref[...], g_ref[...], sc_ref[...], sh_ref[...])
    _store_token_tiles(h_ref, h)
    logits = _mm_nt_f32(wr_ref[...], h) + br_ref[...]
    e_iota = lax.broadcasted_iota(I32, logits.shape, 0).astype(F32)
    vals, idxs, sels = [], [], []
    cur = logits
    for _ in range(TOP_K):
        m = jnp.max(cur, axis=0, keepdims=True)
        idx = jnp.min(jnp.where(cur == m, e_iota, float(N_EXPERTS)), axis=0, keepdims=True)
        sel = e_iota == idx
        cur = jnp.where(sel, -jnp.inf, cur)
        vals.append(m)
        idxs.append(idx.astype(I32))
        sels.append(sel)
    exps = [jnp.exp(v - vals[0]) for v in vals]
    denom = exps[0] + exps[1] + exps[2] + exps[3]
    onehot = jnp.zeros(logits.shape, F32)
    for sel in sels:
        onehot = onehot + jnp.where(sel, 1.0, 0.0)
    row = lax.broadcasted_iota(I32, (tm, tm), 0)
    col = lax.broadcasted_iota(I32, (tm, tm), 1)
    before = _mm(onehot, jnp.where(row < col, 1.0, 0.0)) + carry_ref[...]
    zeros_i = jnp.zeros((GATE_ROWS - TOP_K, tm), I32)
    idx_ref[...] = jnp.concatenate(idxs + [zeros_i], axis=0)
    gate_ref[...] = jnp.concatenate([e / denom for e in exps] + [zeros_i.astype(F32)], axis=0)
    ranks = [jnp.sum(jnp.where(sel, before, 0.0), axis=0, keepdims=True).astype(I32) for sel in sels]
    rank_ref[...] = jnp.concatenate(ranks + [zeros_i], axis=0)
    carry_ref[...] = carry_ref[...] + jnp.sum(onehot, axis=1, keepdims=True)
    cnt_ref[...] = jnp.broadcast_to(carry_ref[...], cnt_ref.shape).astype(I32)


def _router(x2, scale, shift, gain, w_router_t, b_router, *, seq, tm):
    n, d = x2.shape
    per_batch = seq // tm
    batch_spec = pl.BlockSpec((None, 1, d), lambda i: (i // per_batch, 0, 0))
    lane_out = pl.BlockSpec((GATE_ROWS, tm), lambda i: (0, i))
    return pl.pallas_call(
        _router_kernel,
        out_shape=[
            jax.ShapeDtypeStruct((n * SUBLANES, LANES), F32),
            jax.ShapeDtypeStruct((GATE_ROWS, n), I32),
            jax.ShapeDtypeStruct((GATE_ROWS, n), F32),
            jax.ShapeDtypeStruct((GATE_ROWS, n), I32),
            jax.ShapeDtypeStruct((N_EXPERTS, LANES), I32),
        ],
        grid=(n // tm,),
        in_specs=[
            pl.BlockSpec((tm, d), lambda i: (i, 0)),
            batch_spec, batch_spec,
            pl.BlockSpec((1, d), lambda i: (0, 0)),
            pl.BlockSpec((N_EXPERTS, d), lambda i: (0, 0)),
            pl.BlockSpec((N_EXPERTS, 1), lambda i: (0, 0)),
        ],
        out_specs=[
            pl.BlockSpec((tm * SUBLANES, LANES), lambda i: (i, 0)),
            lane_out, lane_out, lane_out,
            pl.BlockSpec((N_EXPERTS, LANES), lambda i: (0, 0)),
        ],
        scratch_shapes=[pltpu.VMEM((N_EXPERTS, 1), F32)],
        compiler_params=_params("arbitrary"),
        name="router",
    )(x2, scale, shift, gain, w_router_t, b_router.reshape(N_EXPERTS, 1))


def _store_token_tiles(ref, value):
    rows = value.shape[0]
    for s in range(SUBLANES):
        ref[pl.ds(s, rows, stride=SUBLANES), :] = value[:, s * LANES:(s + 1) * LANES]


def _load_token_tiles(ref, rows):
    return jnp.concatenate([ref[pl.ds(s, rows, stride=SUBLANES), :] for s in range(SUBLANES)],
                           axis=1)


def _tile_copy(src, src_row, dst, dst_row, sem):
    src_at = pl.multiple_of(src_row * SUBLANES, SUBLANES)
    dst_at = pl.multiple_of(dst_row * SUBLANES, SUBLANES)
    return pltpu.make_async_copy(src.at[pl.ds(src_at, SUBLANES), :],
                                 dst.at[pl.ds(dst_at, SUBLANES), :], sem)


def _load_positions(pos_hbm, pos_smem, sem, count):
    start = pl.multiple_of(pl.program_id(0) * count, count)
    cp = pltpu.make_async_copy(pos_hbm.at[pl.ds(start, count)], pos_smem, sem)
    cp.start()
    cp.wait()


def _dispatch_kernel(meta_ref, pos_hbm, h_ref, xs_hbm, pos_smem, zero_ref, pos_sem, row_sem):
    td = pos_smem.shape[0] // TOP_K
    n_blocks = xs_hbm.shape[0] // (MOE_BLOCK * SUBLANES)
    block_rows = MOE_BLOCK * SUBLANES

    @pl.when(pl.program_id(0) == 0)
    def _():
        zero_ref[...] = jnp.zeros_like(zero_ref)

        def fill(block):
            at = pl.multiple_of(block * block_rows, block_rows)
            return pltpu.make_async_copy(zero_ref, xs_hbm.at[pl.ds(at, block_rows), :], row_sem)

        def each_fill(act):
            for e in range(N_EXPERTS):
                @pl.when(meta_ref[e] > 0)
                def _():
                    act(fill(meta_ref[N_EXPERTS + e] // MOE_BLOCK - 1))

                @pl.when(meta_ref[2 * N_EXPERTS] + e < n_blocks)
                def _():
                    act(fill(meta_ref[2 * N_EXPERTS] + e))

        each_fill(lambda cp: cp.start())
        each_fill(lambda cp: cp.wait())

    _load_positions(pos_hbm, pos_smem, pos_sem, TOP_K * td)

    def start(g, carry):
        t0 = g * DMA_GROUP
        slots = [pos_smem[TOP_K * t0 + j] for j in range(TOP_K * DMA_GROUP)]
        for j, slot in enumerate(slots):
            _tile_copy(h_ref, t0 + j // TOP_K, xs_hbm, slot, row_sem).start(priority=j % 2)
        return carry

    def wait(g, carry):
        for _ in range(TOP_K * DMA_GROUP):
            _tile_copy(h_ref, 0, xs_hbm, 0, row_sem).wait()
        return carry

    lax.fori_loop(0, td // DMA_GROUP, start, 0)
    lax.fori_loop(0, td // DMA_GROUP, wait, 0)


def _dispatch(meta, pos_flat, h3, n_rows, *, td):
    n = h3.shape[0] // SUBLANES
    grid_spec = pltpu.PrefetchScalarGridSpec(
        num_scalar_prefetch=1,
        grid=(n // td,),
        in_specs=[pl.BlockSpec(memory_space=pl.ANY),
                  pl.BlockSpec((td * SUBLANES, LANES), lambda i, meta: (i, 0))],
        out_specs=pl.BlockSpec(memory_space=pl.ANY),
        scratch_shapes=[
            pltpu.SMEM((TOP_K * td,), I32),
            pltpu.VMEM((MOE_BLOCK * SUBLANES, LANES), F32),
            pltpu.SemaphoreType.DMA,
            pltpu.SemaphoreType.DMA,
        ],
    )
    return pl.pallas_call(
        _dispatch_kernel,
        out_shape=jax.ShapeDtypeStruct((n_rows * SUBLANES, LANES), F32),
        grid_spec=grid_spec,
        compiler_params=_params("arbitrary"),
        name="moe_dispatch",
    )(meta, pos_flat, h3)


PAIR_TILE = 2 * LANES
EXPERT_CHUNK = 256


def _pair_regroup_kernel(w_ref, o_ref):
    r = lax.broadcasted_iota(I32, (PAIR_TILE, PAIR_TILE), 0)
    c = lax.broadcasted_iota(I32, (PAIR_TILE, PAIR_TILE), 1)
    src = jnp.where(c < LANES, 2 * c, 2 * (c - LANES) + 1)
    perm = jnp.where(r == src, 1.0, 0.0).astype(BF16)
    for t in range(w_ref.shape[1] // PAIR_TILE):
        cols = slice(t * PAIR_TILE, (t + 1) * PAIR_TILE)
        o_ref[:, cols] = _mm(w_ref[:, cols], perm).astype(o_ref.dtype)


def _pair_regroup(w_all, layer, *, tr):
    _, e, d, width = w_all.shape
    return pl.pallas_call(
        _pair_regroup_kernel,
        out_shape=jax.ShapeDtypeStruct((e, d, width), BF16),
        grid=(e, d // tr),
        in_specs=[pl.BlockSpec((None, None, tr, width), lambda i, j: (layer, i, j, 0))],
        out_specs=pl.BlockSpec((None, tr, width), lambda i, j: (i, j, 0)),
        compiler_params=_params("arbitrary", "arbitrary"),
        name="gate_up_regroup",
    )(w_all)


def _expert_kernel(be_ref, x_ref, wgu_ref, bgu_ref, wd_ref, bd_ref, o_ref):
    f = wd_ref.shape[0]
    n_chunks = f // EXPERT_CHUNK
    gu_cols = 2 * EXPERT_CHUNK
    used = pl.program_id(0) < be_ref[pl.num_programs(0)]

    @pl.when(used)
    def _():
        x = _load_token_tiles(x_ref, MOE_BLOCK).astype(BF16)

        def gate_up(c):
            cols = slice(c * gu_cols, (c + 1) * gu_cols)
            return _mm(x, wgu_ref[:, cols]) + bgu_ref[:, cols]

        y = bd_ref[...]
        gu = gate_up(0)
        for c in range(n_chunks):
            gu_next = gate_up(c + 1) if c + 1 < n_chunks else None
            acts = []
            for t in range(EXPERT_CHUNK // LANES):
                gate = jnp.minimum(gu[:, t * PAIR_TILE:t * PAIR_TILE + LANES], SWIGLU_LIMIT)
                up = jnp.clip(gu[:, t * PAIR_TILE + LANES:(t + 1) * PAIR_TILE], -SWIGLU_LIMIT,
                              SWIGLU_LIMIT)
                acts.append(((up + 1.0) * gate * _sigmoid(SWIGLU_ALPHA * gate)).astype(BF16))
            y = y + _mm(jnp.concatenate(acts, axis=1),
                        wd_ref[c * EXPERT_CHUNK:(c + 1) * EXPERT_CHUNK, :])
            gu = gu_next
        _store_token_tiles(o_ref, y)

    @pl.when(jnp.logical_not(used))
    def _():
        o_ref[...] = jnp.zeros_like(o_ref)


def _experts(block_e, xs, wgu, bgu, wd_all, bd, layer):
    f, d = wd_all.shape[2:]
    block_rows = MOE_BLOCK * SUBLANES
    n_blocks = xs.shape[0] // block_rows
    grid_spec = pltpu.PrefetchScalarGridSpec(
        num_scalar_prefetch=1,
        grid=(n_blocks,),
        in_specs=[
            pl.BlockSpec((block_rows, LANES), lambda i, be: (i, 0)),
            pl.BlockSpec((None, d, 2 * f), lambda i, be: (be[i], 0, 0)),
            pl.BlockSpec((None, 1, 2 * f), lambda i, be: (be[i], 0, 0)),
            pl.BlockSpec((None, None, f, d), lambda i, be: (layer, be[i], 0, 0)),
            pl.BlockSpec((None, 1, d), lambda i, be: (be[i], 0, 0)),
        ],
        out_specs=pl.BlockSpec((block_rows, LANES), lambda i, be: (i, 0)),
    )
    return pl.pallas_call(
        _expert_kernel,
        out_shape=jax.ShapeDtypeStruct(xs.shape, F32),
        grid_spec=grid_spec,
        compiler_params=_params("arbitrary"),
        name="moe_experts",
    )(block_e, xs, wgu, bgu, wd_all, bd)


def _combine_kernel(pos_hbm, ys_hbm, gate_ref, x_ref, gf_ref, *rest, final_norm):
    if final_norm:
        ng_ref, o_ref, buf_ref, pos_smem, pos_sem, row_sem = rest
    else:
        o_ref, buf_ref, pos_smem, pos_sem, row_sem = rest
    tc = x_ref.shape[0]
    per_step = TOP_K * tc
    n_stage = pos_smem.shape[0] // per_step
    stage = lax.rem(pl.program_id(0), n_stage)

    @pl.when(stage == 0)
    def _():
        count = n_stage * per_step
        at = pl.multiple_of((pl.program_id(0) // n_stage) * count, count)
        cp = pltpu.make_async_copy(pos_hbm.at[pl.ds(at, count)], pos_smem, pos_sem)
        cp.start()
        cp.wait()

    base = stage * per_step
    half = tc // 2
    groups = half // DMA_GROUP

    def gather(part):
        sem = row_sem.at[part]

        def start(g, carry):
            t0 = part * half + g * DMA_GROUP
            slots = [pos_smem[base + TOP_K * t0 + j] for j in range(TOP_K * DMA_GROUP)]
            for j, slot in enumerate(slots):
                _tile_copy(ys_hbm, slot, buf_ref.at[j % TOP_K], t0 + j // TOP_K,
                           sem).start(priority=j % 2)
            return carry

        lax.fori_loop(0, groups, start, 0)

    def finish(part):
        sem = row_sem.at[part]

        def wait(g, carry):
            for _ in range(TOP_K * DMA_GROUP):
                _tile_copy(ys_hbm, 0, buf_ref.at[0], 0, sem).wait()
            return carry

        lax.fori_loop(0, groups, wait, 0)
        rows = slice(part * half, (part + 1) * half)
        gates = [jnp.broadcast_to(gate_ref[rows, k:k + 1], (half, LANES)) for k in range(TOP_K)]
        ys = []
        for s in range(SUBLANES):
            y = None
            for k in range(TOP_K):
                term = gates[k] * buf_ref[k, pl.ds(part * half * SUBLANES + s, half,
                                                   stride=SUBLANES), :]
                y = term if y is None else y + term
            ys.append(y)
        out = x_ref[rows, :] + gf_ref[...] * jnp.concatenate(ys, axis=1)
        if final_norm:
            out = _rms(out) * ng_ref[...]
        o_ref[rows, :] = out

    gather(0)
    gather(1)
    finish(0)
    finish(1)


def _combine(pos_flat, ys, gates, x2, gf, final_gain, *, seq, tc):
    n, d = x2.shape
    per_batch = seq // tc
    final_norm = final_gain is not None
    in_specs = [
        pl.BlockSpec(memory_space=pl.ANY),
        pl.BlockSpec(memory_space=pl.ANY),
        pl.BlockSpec((tc, TOP_K), lambda i: (i, 0)),
        pl.BlockSpec((tc, d), lambda i: (i, 0)),
        pl.BlockSpec((None, 1, d), lambda i: (i // per_batch, 0, 0)),
    ]
    args = [pos_flat, ys, gates, x2, gf]
    if final_norm:
        in_specs.append(pl.BlockSpec((1, d), lambda i: (0, 0)))
        args.append(final_gain.reshape(1, d))
    return pl.pallas_call(
        functools.partial(_combine_kernel, final_norm=final_norm),
        out_shape=jax.ShapeDtypeStruct((n, d), F32),
        grid=(n // tc,),
        in_specs=in_specs,
        out_specs=pl.BlockSpec((tc, d), lambda i: (i, 0)),
        scratch_shapes=[
            pltpu.VMEM((TOP_K, tc * SUBLANES, LANES), F32),
            pltpu.SMEM((math.gcd(n // tc, POS_STAGE) * TOP_K * tc,), I32),
            pltpu.SemaphoreType.DMA,
            pltpu.SemaphoreType.DMA((2,)),
        ],
        compiler_params=_params("arbitrary"),
        name="moe_combine",
    )(*args)


def _split_w_in(w_in):
    d = w_in.shape[0]
    sizes = (GLA_QK, GLA_QK, GLA_V, GLA_GATE_RANK, GLA_V, GDN_QKV, GDN_HEADS, GDN_HEADS, GDN_V,
             SSD_INNER, SSD_XBC, SSD_HEADS, N_BRANCH * d)
    parts, acc = [], 0
    for s in sizes:
        parts.append(w_in[:, acc:acc + s])
        acc += s
    (gq, gk, gv, glr, gr, dqkv, da, db, dg, sz, sxbc, sdt, mrg) = parts
    zeros = lambda w: jnp.zeros((d, w), w_in.dtype)
    w_gla = jnp.concatenate([gq, gk, gv, gr, glr, zeros(LANES - GLA_GATE_RANK)], axis=1)
    w_gdn = jnp.concatenate([dqkv, dg, da, db, zeros(LANES - 2 * GDN_HEADS)], axis=1)
    w_ssd = jnp.concatenate([sz, sxbc, sdt, zeros(LANES - SSD_HEADS)], axis=1)
    pad_rows = lambda m: jnp.pad(m, ((0, GATE_ROWS - m.shape[0]), (0, 0)))
    w_gdn_t = pad_rows(jnp.concatenate([da, db], axis=1).T)
    w_ssd_t = pad_rows(sdt.T)
    cast = lambda m: m.astype(BF16)
    return cast(w_gla), cast(w_gdn), cast(w_gdn_t), cast(w_ssd), cast(w_ssd_t), cast(mrg)


def _moe(x2, scale, shift, gain, gf, final_gain, layer, w_router, b_router, w_gate_up, b_gate_up,
         w_down, b_down, *, seq, tm):
    n, d = x2.shape
    h3, idx_t, gate_t, rank_t, counts = _router(x2, scale, shift, gain, w_router.T, b_router,
                                                seq=seq, tm=tm)
    counts = counts[:, 0]
    padded = ((counts + MOE_BLOCK - 1) // MOE_BLOCK) * MOE_BLOCK
    pad_end = jnp.cumsum(padded)
    pad_start = pad_end - padded
    n_blocks = -(-(n * TOP_K) // MOE_BLOCK) + N_EXPERTS
    n_used = pad_end[-1:] // MOE_BLOCK
    meta = jnp.concatenate([padded, pad_end, n_used]).astype(I32)
    block_start = jnp.arange(n_blocks, dtype=I32) * MOE_BLOCK
    block_e = jnp.minimum(jnp.sum((pad_end[None, :] <= block_start[:, None]).astype(I32), axis=1),
                          N_EXPERTS - 1)
    block_e = jnp.concatenate([block_e, n_used]).astype(I32)
    experts = jnp.arange(N_EXPERTS, dtype=I32)
    start_of = jnp.sum(jnp.where(idx_t[:TOP_K, :, None] == experts, pad_start, 0), axis=-1)
    pos = (start_of + rank_t[:TOP_K]).astype(I32)
    pos_flat = pos.T.reshape(-1)
    gates = gate_t[:TOP_K].T
    f = w_down.shape[2]
    wgu = _pair_regroup(w_gate_up, layer, tr=min(512, d))
    bgu = b_gate_up.reshape(N_EXPERTS, 2 * f // PAIR_TILE, LANES, 2).transpose(0, 1, 3, 2)
    xs = _dispatch(meta, pos_flat, h3, n_blocks * MOE_BLOCK, td=min(DISPATCH_TILE, n))
    ys = _experts(block_e, xs, wgu, bgu.reshape(N_EXPERTS, 1, 2 * f), w_down,
                  b_down.reshape(N_EXPERTS, 1, d), layer)
    return _combine(pos_flat, ys, gates, x2, gf, final_gain, seq=seq, tc=min(COMBINE_TILE, seq))


def kernel(x, c, w_mod, b_mod, norm_mix, norm_ffn, norm_final, w_in, gla_w_gate2, gla_b_gate2, gla_norm, gdn_conv_w, gdn_a_log, gdn_dt_bias, gdn_norm, ssd_conv_w, ssd_conv_b, ssd_a_log, ssd_dt_bias, ssd_d, ssd_norm, w_branch_gla, w_branch_gdn, w_branch_ssd, b_merge, w_out, w_router, b_router, w_gate_up, b_gate_up, w_down, b_down):
    bsz, seq, d = x.shape
    depth = w_mod.shape[0]
    n = bsz * seq
    tm = min(512, seq)
    mod = _modulation(c, w_mod, b_mod)
    x2 = x.reshape(n, d)
    for l in range(depth):
        sh_m, sc_m, g_m, sh_f, sc_f, g_f = [mod[l, :, i * d:(i + 1) * d].reshape(bsz, 1, d)
                                            for i in range(6)]
        w_gla, w_gdn, w_gdn_t, w_ssd, w_ssd_t, w_mrg = _split_w_in(w_in[l])
        gain = norm_mix[l].reshape(1, d)
        proj = functools.partial(_inproj, x2, sc_m, sh_m, gain, seq=seq, tm=tm)
        p_gla = proj(w_gla, None)
        p_gdn, abt = proj(w_gdn, w_gdn_t)
        p_ssd, dtt = proj(w_ssd, w_ssd_t)
        p_mrg = proj(w_mrg, None)
        w2 = jnp.pad(gla_w_gate2[l], ((0, LANES - GLA_GATE_RANK), (0, 0))).astype(BF16)
        tile = min(MIXER_TILE, seq)
        y_gla = _gla(p_gla, w2, gla_b_gate2[l].reshape(1, GLA_QK), gla_norm[l].reshape(1, GLA_DV),
                     bsz=bsz, seq=seq, tile=tile)
        y_gdn = _gdn(p_gdn, abt, gdn_conv_w[l], gdn_a_log[l], gdn_dt_bias[l], gdn_norm[l],
                     bsz=bsz, seq=seq, tile=tile)
        y_ssd = _ssd(p_ssd, dtt, ssd_conv_w[l], ssd_conv_b[l], ssd_a_log[l],
                     ssd_dt_bias[l], ssd_d[l], ssd_norm[l], bsz=bsz, seq=seq, tile=tile)
        x2 = _merge(x2, y_gla, y_gdn, y_ssd, p_mrg, b_merge[l],
                    w_branch_gla[l].astype(BF16), w_branch_gdn[l].astype(BF16),
                    w_branch_ssd[l].astype(BF16), w_out[l].astype(BF16), g_m, seq=seq, tm=tm)
        final_gain = norm_final if l == depth - 1 else None
        x2 = _moe(x2, sc_f, sh_f, norm_ffn[l].reshape(1, d), g_f, final_gain, l, w_router[l],
                  b_router[l], w_gate_up, b_gate_up[l], w_down, b_down[l], seq=seq, tm=tm)
    return x2.reshape(bsz, seq, d)
```
